```python
import jax, jax.numpy as jnp
from jax import lax
import numpy as np

D_MODEL = 1024
BATCH = 8
SEQ = 2048
DEPTH = 2
DEC_BATCH = 1
DEC_SEQ = 16384
PAST_LEN = 128

GRID_W = 64
N_MIXERS = 2
NA_HEADS = 16
NA_HEAD_DIM = 64
NA_WIN_ROWS = 8
NA_WIN_COLS = 16
NA_COL_BLOCK = 16
NA_KEY_COLS = 32
GQA_Q_HEADS = 8
GQA_KV_HEADS = 4
GQA_GROUP = GQA_Q_HEADS // GQA_KV_HEADS
GQA_HEAD_DIM = 128
GQA_Q_BLOCK = 128
ROPE_THETA = 10000.0
PLE_DIM = 256
D_FF = ((8 * D_MODEL // 3 + 255) // 256) * 256
EPS = 1e-6
NEG_INF = -1e30

kernel_name = "hybrid_natten_gqa_sandwich_ple_encoder"


def _rms(x):
    xf = x.astype(jnp.float32)
    return xf * lax.rsqrt(jnp.mean(xf * xf, axis=-1, keepdims=True) + EPS)


def rms_norm(x, g):
    return (_rms(x) * g.astype(jnp.float32)).astype(x.dtype)


def _na_static_tables():
    n_cb = GRID_W // NA_COL_BLOCK
    qcol = np.arange(GRID_W).reshape(n_cb, NA_COL_BLOCK)
    kstart = np.clip(np.arange(n_cb) * NA_COL_BLOCK - NA_WIN_COLS // 2, 0, GRID_W - NA_KEY_COLS)
    kcol = kstart[:, None] + np.arange(NA_KEY_COLS)
    wstart = np.clip(qcol - NA_WIN_COLS // 2, 0, GRID_W - NA_WIN_COLS)
    kc = kcol[:, None, :]
    col_mask = (kc >= wstart[..., None]) & (kc < wstart[..., None] + NA_WIN_COLS)
    col_off = np.clip(kc - qcol[..., None] + NA_WIN_COLS - 1, 0, 2 * NA_WIN_COLS - 2)
    return n_cb, kcol, col_mask, col_off


def neighborhood_attention(h, w_qkv, rpb, w_o):
    B, L, _ = h.shape
    rows = L // GRID_W
    kr = min(NA_WIN_ROWS, rows)
    n_cb, kcol, col_mask, col_off = _na_static_tables()
    qkv = (h @ w_qkv).reshape(B, rows, GRID_W, 3, NA_HEADS, NA_HEAD_DIM)
    q = qkv[:, :, :, 0] * (NA_HEAD_DIM ** -0.5)
    k = qkv[:, :, :, 1]
    v = qkv[:, :, :, 2]
    qcb = q.reshape(B, rows, n_cb, NA_COL_BLOCK, NA_HEADS, NA_HEAD_DIM)
    kcb = k[:, :, kcol]
    vcb = v[:, :, kcol]
    rpb_cols = rpb[:, :, col_off]
    mask = jnp.asarray(col_mask)[:, :, None, :]

    def row_fn(r):
        rs = jnp.clip(r - kr // 2, 0, rows - kr)
        kb = lax.dynamic_slice_in_dim(kcb, rs, kr, axis=1)
        vb = lax.dynamic_slice_in_dim(vcb, rs, kr, axis=1)
        qr = lax.dynamic_index_in_dim(qcb, r, axis=1, keepdims=False)
        s = jnp.einsum('bjqhd,bujkhd->bhjquk', qr, kb, preferred_element_type=jnp.float32)
        row_idx = rs + jnp.arange(kr) - r + NA_WIN_ROWS - 1
        bias = jnp.take(rpb_cols, row_idx, axis=1).astype(jnp.float32)
        bias = bias.transpose(0, 2, 3, 1, 4)
        s = jnp.where(mask, s + bias[None], NEG_INF)
        s_shape = s.shape
        p = jax.nn.softmax(s.reshape(s_shape[:4] + (kr * NA_KEY_COLS,)), axis=-1).reshape(s_shape)
        return jnp.einsum('bhjquk,bujkhd->bjqhd', p.astype(vb.dtype), vb)

    o = lax.map(row_fn, jnp.arange(rows))
    o = o.transpose(1, 0, 2, 3, 4, 5).reshape(B, L, NA_HEADS * NA_HEAD_DIM)
    return o @ w_o


def _axial_rope(L):
    t = jnp.arange(L)
    row = (t // GRID_W).astype(jnp.float32)
    col = (t % GRID_W).astype(jnp.float32)
    axis_dim = GQA_HEAD_DIM // 2
    inv_freq = ROPE_THETA ** (-jnp.arange(0, axis_dim, 2, dtype=jnp.float32) / axis_dim)
    ang = jnp.stack([row[:, None] * inv_freq, col[:, None] * inv_freq], axis=1)
    return jnp.cos(ang), jnp.sin(ang)


def _apply_rope(x, cos, sin):
    B, L, H, _ = x.shape
    n_f = GQA_HEAD_DIM // 4
    xr = x.astype(jnp.float32).reshape(B, L, H, 2, 2, n_f)
    a, b = xr[..., 0, :], xr[..., 1, :]
    c = cos[None, :, None]
    s = sin[None, :, None]
    out = jnp.stack([a * c - b * s, b * c + a * s], axis=-2)
    return out.reshape(B, L, H, GQA_HEAD_DIM).astype(x.dtype)


def gqa_attention(h, w_qkv, q_norm, k_norm, w_o):
    B, L, _ = h.shape
    qd = GQA_Q_HEADS * GQA_HEAD_DIM
    kd = GQA_KV_HEADS * GQA_HEAD_DIM
    qkv = h @ w_qkv
    q = qkv[..., :qd].reshape(B, L, GQA_Q_HEADS, GQA_HEAD_DIM)
    k = qkv[..., qd:qd + kd].reshape(B, L, GQA_KV_HEADS, GQA_HEAD_DIM)
    v = qkv[..., qd + kd:].reshape(B, L, GQA_KV_HEADS, GQA_HEAD_DIM)
    cos, sin = _axial_rope(L)
    q = _apply_rope(rms_norm(q, q_norm), cos, sin) * (GQA_HEAD_DIM ** -0.5)
    k = _apply_rope(rms_norm(k, k_norm), cos, sin)
    nb = L // GQA_Q_BLOCK
    qb = q.reshape(B, nb, GQA_Q_BLOCK, GQA_KV_HEADS, GQA_GROUP, GQA_HEAD_DIM).transpose(1, 0, 2, 3, 4, 5)

    def blk(qi):
        s = jnp.einsum('bqkgd,bskd->bkgqs', qi, k, preferred_element_type=jnp.float32)
        p = jax.nn.softmax(s, axis=-1).astype(v.dtype)
        return jnp.einsum('bkgqs,bskd->bqkgd', p, v)

    o = lax.map(blk, qb)
    o = o.transpose(1, 0, 2, 3, 4, 5).reshape(B, L, GQA_Q_HEADS * GQA_HEAD_DIM)
    return o @ w_o


def swiglu(h, w_gate_up, w_down):
    gu = h @ w_gate_up
    g, u = gu[..., :D_FF], gu[..., D_FF:]
    return (jax.nn.silu(g) * u) @ w_down


def trunk(x, p, mix_pre_norm, mix_post_norm, ffn_pre_norm, ffn_post_norm,
          na_w_qkv, na_rpb, na_w_o, gqa_w_qkv, gqa_q_norm, gqa_k_norm, gqa_w_o,
          ffn_w_gate_up, ffn_w_down, ple_w_gate, ple_w_proj):
    h = x
    for i in range(DEPTH):
        j = i // N_MIXERS
        a = rms_norm(h, mix_pre_norm[i])
        if i % N_MIXERS == 0:
            m = neighborhood_attention(a, na_w_qkv[j], na_rpb[j], na_w_o[j])
        else:
            m = gqa_attention(a, gqa_w_qkv[j], gqa_q_norm[j], gqa_k_norm[j], gqa_w_o[j])
        h = h + rms_norm(m, mix_post_norm[i])
        f = swiglu(rms_norm(h, ffn_pre_norm[i]), ffn_w_gate_up[i], ffn_w_down[i])
        h = h + rms_norm(f, ffn_post_norm[i])
        gate = jax.nn.sigmoid(_rms(h).astype(h.dtype) @ ple_w_gate[i])
        h = h + (p[i] @ ple_w_proj[i]) * gate
    return h


def setup_inputs(seed: int = 0) -> dict:
    key = jax.random.key(seed)
    ks = jax.random.split(key, 20)
    n_a = (DEPTH + 1) // 2
    n_b = DEPTH // 2
    f32 = jnp.float32
    nrm = lambda k, shape, scale: jax.random.normal(k, shape, f32) * scale
    gain = lambda k, shape: 1.0 + 0.05 * jax.random.normal(k, shape, f32)
    na_w = NA_HEADS * NA_HEAD_DIM
    gqa_w = (GQA_Q_HEADS + 2 * GQA_KV_HEADS) * GQA_HEAD_DIM
    return {
        "x_prompt": jax.random.normal(ks[0], (BATCH, SEQ, D_MODEL), f32),
        "x_sample": jax.random.normal(ks[1], (DEC_BATCH, DEC_SEQ, D_MODEL), f32),
        "p_prompt": jax.random.normal(ks[2], (DEPTH, BATCH, SEQ, PLE_DIM), f32),
        "p_sample": jax.random.normal(ks[3], (DEPTH, DEC_BATCH, DEC_SEQ, PLE_DIM), f32),
        "mix_pre_norm": gain(ks[4], (DEPTH, D_MODEL)),
        "mix_post_norm": gain(ks[5], (DEPTH, D_MODEL)),
        "ffn_pre_norm": gain(ks[6], (DEPTH, D_MODEL)),
        "ffn_post_norm": gain(ks[7], (DEPTH, D_MODEL)),
        "na_w_qkv": nrm(ks[8], (n_a, D_MODEL, 3 * na_w), D_MODEL ** -0.5),
        "na_rpb": nrm(ks[9], (n_a, NA_HEADS, 2 * NA_WIN_ROWS - 1, 2 * NA_WIN_COLS - 1), 0.5),
        "na_w_o": nrm(ks[10], (n_a, na_w, D_MODEL), na_w ** -0.5),
        "gqa_w_qkv": nrm(ks[11], (n_b, D_MODEL, gqa_w), D_MODEL ** -0.5),
        "gqa_q_norm": gain(ks[12], (n_b, GQA_HEAD_DIM)),
        "gqa_k_norm": gain(ks[13], (n_b, GQA_HEAD_DIM)),
        "gqa_w_o": nrm(ks[14], (n_b, GQA_Q_HEADS * GQA_HEAD_DIM, D_MODEL), (GQA_Q_HEADS * GQA_HEAD_DIM) ** -0.5),
        "ffn_w_gate_up": nrm(ks[15], (DEPTH, D_MODEL, 2 * D_FF), D_MODEL ** -0.5),
        "ffn_w_down": nrm(ks[16], (DEPTH, D_FF, D_MODEL), D_FF ** -0.5),
        "ple_w_gate": nrm(ks[17], (DEPTH, D_MODEL, D_MODEL), D_MODEL ** -0.5),
        "ple_w_proj": nrm(ks[18], (DEPTH, PLE_DIM, D_MODEL), PLE_DIM ** -0.5),
    }


def reference(x_prompt, x_sample, p_prompt, p_sample, mix_pre_norm, mix_post_norm,
              ffn_pre_norm, ffn_post_norm, na_w_qkv, na_rpb, na_w_o, gqa_w_qkv,
              gqa_q_norm, gqa_k_norm, gqa_w_o, ffn_w_gate_up, ffn_w_down,
              ple_w_gate, ple_w_proj):
    y_prompt = trunk(x_prompt, p_prompt, mix_pre_norm, mix_post_norm, ffn_pre_norm, ffn_post_norm,
                     na_w_qkv, na_rpb, na_w_o, gqa_w_qkv, gqa_q_norm, gqa_k_norm, gqa_w_o,
                     ffn_w_gate_up, ffn_w_down, ple_w_gate, ple_w_proj)
    y_sample = trunk(x_sample, p_sample, mix_pre_norm, mix_post_norm, ffn_pre_norm, ffn_post_norm,
                     na_w_qkv, na_rpb, na_w_o, gqa_w_qkv, gqa_q_norm, gqa_k_norm, gqa_w_o,
                     ffn_w_gate_up, ffn_w_down, ple_w_gate, ple_w_proj)
    return (y_prompt, y_sample)
```

```python
import functools
import math

import jax
import jax.numpy as jnp
import numpy as np
from jax import lax
from jax.experimental import pallas as pl
from jax.experimental.pallas import tpu as pltpu

F32 = jnp.float32
BF16 = jnp.bfloat16

D_MODEL = 1024
GRID_W = 64
NA_HEADS = 16
NA_HEAD_DIM = 64
NA_WIN_ROWS = 8
NA_WIN_COLS = 16
GQA_Q_HEADS = 8
GQA_KV_HEADS = 4
GQA_HEAD_DIM = 128
ROPE_THETA = 10000.0
PLE_DIM = 256
D_FF = 2816
EPS = 1e-6
NEG_INF = -1e30
LOG2E = math.log2(math.e)

LANES = 128
VMEM_LIMIT = 56 * 1024 * 1024

NA_Q_ROWS = 4
NA_K_ROWS = 12
NA_TQ = NA_Q_ROWS * GRID_W
NA_TK = NA_K_ROWS * GRID_W
NA_PAIRS = NA_HEADS // 2


def _rms(x):
    return x * lax.rsqrt(jnp.mean(x * x, axis=-1, keepdims=True) + EPS)


def _sigmoid(x):
    return 1.0 / (1.0 + jnp.exp(-x))


def _dot(a, b):
    return jnp.dot(a, b, preferred_element_type=F32)


def _dot_nt(a, b):
    return lax.dot_general(a, b, (((1,), (1,)), ((), ())), preferred_element_type=F32)


def _resident(shape):
    return pl.BlockSpec(shape, lambda *_: (0,) * len(shape), pipeline_mode=pl.Buffered(1))


def _pre_na_kernel(x_ref, g_ref, w_ref, o_ref):
    a = (_rms(x_ref[...]) * g_ref[...]).astype(BF16)
    qkv = _dot(a, w_ref[...])
    n_q = NA_PAIRS
    q_scale = NA_HEAD_DIM ** -0.5 * LOG2E
    for c in range(3 * NA_PAIRS):
        piece = qkv[:, c * LANES:(c + 1) * LANES]
        if c < n_q:
            piece = piece * q_scale
        o_ref[c] = piece.astype(BF16)


def _pre_na(x, gain, w_qkv, tm=512):
    t = x.shape[0]
    return pl.pallas_call(
        _pre_na_kernel,
        grid=(t // tm,),
        in_specs=[
            pl.BlockSpec((tm, D_MODEL), lambda i: (i, 0)),
            _resident((1, D_MODEL)),
            _resident((D_MODEL, 3 * D_MODEL)),
        ],
        out_specs=pl.BlockSpec((3 * NA_PAIRS, tm, LANES), lambda i: (0, i, 0)),
        out_shape=jax.ShapeDtypeStruct((3 * NA_PAIRS, t, LANES), BF16),
        compiler_params=pltpu.CompilerParams(
            dimension_semantics=("parallel",), vmem_limit_bytes=VMEM_LIMIT),
        name="pre_na",
    )(x, gain, w_qkv)


def _na_bias_table(rpb):
    rq = np.arange(NA_Q_ROWS)[:, None, None, None]
    qc = np.arange(GRID_W)[None, :, None, None]
    u = np.arange(NA_K_ROWS)[None, None, :, None]
    kc = np.arange(GRID_W)[None, None, None, :]
    wstart = np.clip(qc - NA_WIN_COLS // 2, 0, GRID_W - NA_WIN_COLS)
    col_ok = (kc >= wstart) & (kc < wstart + NA_WIN_COLS)
    col_off = np.clip(kc - qc + NA_WIN_COLS - 1, 0, 2 * NA_WIN_COLS - 2)
    col_off = np.broadcast_to(col_off, (NA_Q_ROWS, GRID_W, NA_K_ROWS, GRID_W))
    tables = []
    for variant in range(3):
        if variant == 0:
            dr = u - rq + NA_WIN_ROWS - 1
            row_ok = u < NA_WIN_ROWS
        elif variant == 1:
            dr = u - rq + NA_WIN_ROWS - 1 - NA_Q_ROWS
            row_ok = (u >= rq) & (u < rq + NA_WIN_ROWS)
        else:
            dr = u - rq + NA_WIN_ROWS - 1 - 2 * NA_Q_ROWS
            row_ok = u >= NA_K_ROWS - NA_WIN_ROWS
        ok = np.broadcast_to(row_ok & col_ok, (NA_Q_ROWS, GRID_W, NA_K_ROWS, GRID_W))
        dr = np.broadcast_to(np.clip(dr, 0, 2 * NA_WIN_ROWS - 2), ok.shape)
        vals = rpb[:, dr, col_off] * LOG2E
        vals = jnp.where(jnp.asarray(ok)[None], vals, NEG_INF)
        tables.append(vals.reshape(NA_PAIRS, 2 * NA_TQ, NA_TK))
    return jnp.stack(tables)


def _na_kernel(q_ref, k0_ref, k1_ref, k2_ref, v0_ref, v1_ref, v2_ref, bias_ref, o_ref):
    lane = lax.broadcasted_iota(jnp.int32, (1, LANES), 1)
    first_head = lane < NA_HEAD_DIM
    k_refs = (k0_ref, k1_ref, k2_ref)
    v_refs = (v0_ref, v1_ref, v2_ref)
    n_blk = len(k_refs)
    tkb = NA_TK // n_blk

    def pair_body(hp, carry):
        q = q_ref[hp]
        zero = jnp.zeros_like(q)
        q2 = jnp.concatenate([jnp.where(first_head, q, zero), jnp.where(first_head, zero, q)], axis=0)
        s = [_dot_nt(q2, k_refs[j][hp]) + bias_ref[0, hp, :, j * tkb:(j + 1) * tkb] for j in range(n_blk)]
        m = functools.reduce(jnp.maximum, [jnp.max(sj, axis=-1, keepdims=True) for sj in s])
        p = [jnp.exp2(sj - m) for sj in s]
        l = functools.reduce(jnp.add, [jnp.sum(pj, axis=-1, keepdims=True) for pj in p])
        o2 = functools.reduce(jnp.add, [_dot(p[j].astype(BF16), v_refs[j][hp]) for j in range(n_blk)])
        o2 = o2 / l
        o_ref[hp] = jnp.where(first_head, o2[:NA_TQ], o2[NA_TQ:]).astype(BF16)
        return carry

    lax.fori_loop(0, NA_PAIRS, pair_body, 0)


def _na_geometry(i, n_prompt_blocks, prompt_blocks_per_seq, sample_blocks):
    is_prompt = i < n_prompt_blocks
    n_blk = jnp.where(is_prompt, prompt_blocks_per_seq, sample_blocks)
    first = jnp.where(is_prompt, (i // prompt_blocks_per_seq) * prompt_blocks_per_seq, n_prompt_blocks)
    local = i - first
    win = first + jnp.clip(local - 1, 0, n_blk - 3)
    variant = jnp.where(local == 0, 0, jnp.where(local == n_blk - 1, 2, 1))
    return win, variant


def _na_attention(qkv, bias, prompt_tokens, prompt_seq, sample_seq):
    t = qkv.shape[1]
    geom = functools.partial(
        _na_geometry,
        n_prompt_blocks=prompt_tokens // NA_TQ,
        prompt_blocks_per_seq=prompt_seq // NA_TQ,
        sample_blocks=sample_seq // NA_TQ)
    assert prompt_seq // NA_TQ >= 3 and sample_seq // NA_TQ >= 3

    def kv_spec(part, j):
        return pl.BlockSpec((NA_PAIRS, NA_TQ, LANES), lambda i: (part, geom(i)[0] + j, 0))

    return pl.pallas_call(
        _na_kernel,
        grid=(t // NA_TQ,),
        in_specs=[pl.BlockSpec((NA_PAIRS, NA_TQ, LANES), lambda i: (0, i, 0))]
        + [kv_spec(1, j) for j in range(3)] + [kv_spec(2, j) for j in range(3)]
        + [pl.BlockSpec((1, NA_PAIRS, 2 * NA_TQ, NA_TK), lambda i: (geom(i)[1], 0, 0, 0))],
        out_specs=pl.BlockSpec((NA_PAIRS, NA_TQ, LANES), lambda i: (0, i, 0)),
        out_shape=jax.ShapeDtypeStruct((NA_PAIRS, t, LANES), BF16),
        compiler_params=pltpu.CompilerParams(
            dimension_semantics=("parallel",), vmem_limit_bytes=VMEM_LIMIT),
        name="na_attention",
    )(qkv, qkv, qkv, qkv, qkv, qkv, qkv, bias)


def _swap_rope_halves(x, lane_in_first_half):
    return jnp.where(lane_in_first_half, pltpu.roll(x, 96, 1), pltpu.roll(x, 32, 1))


def _post_kernel(*refs, o_is_paired, with_gqa):
    if with_gqa:
        (o_ref, h_ref, p_ref, wo_ref, gpost_ref, gfpre_ref, wgu_ref, wd_ref, gfpost_ref, wpg_ref, wpp_ref,
         gnext_ref, wqkv_ref, qn_ref, kn_ref, cos_ref, sin_ref, h_out, q_out, k_out, v_out) = refs
    else:
        (o_ref, h_ref, p_ref, wo_ref, gpost_ref, gfpre_ref, wgu_ref, wd_ref, gfpost_ref, wpg_ref, wpp_ref,
         h_out) = refs

    if o_is_paired:
        o = jnp.concatenate([o_ref[c] for c in range(NA_PAIRS)], axis=-1)
    else:
        o = o_ref[...]
    h = h_ref[...] + _rms(_dot(o, wo_ref[...])) * gpost_ref[...]

    a = (_rms(h) * gfpre_ref[...]).astype(BF16)
    g = _dot(a, wgu_ref[:, :D_FF])
    u = _dot(a, wgu_ref[:, D_FF:])
    act = (g * _sigmoid(g) * u).astype(BF16)
    h = h + _rms(_dot(act, wd_ref[...])) * gfpost_ref[...]

    gate = _sigmoid(_dot(_rms(h).astype(BF16), wpg_ref[...]))
    h = h + _dot(p_ref[...].astype(BF16), wpp_ref[...]) * gate
    h_out[...] = h

    if with_gqa:
        a = (_rms(h) * gnext_ref[...]).astype(BF16)
        qkv = _dot(a, wqkv_ref[...])
        cos = cos_ref[...]
        sin = sin_ref[...]
        lane = lax.broadcasted_iota(jnp.int32, (1, LANES), 1)
        first_half = (lane % 64) < 32
        q_scale = GQA_HEAD_DIM ** -0.5 * LOG2E
        n_qk = GQA_Q_HEADS + GQA_KV_HEADS
        for c in range(n_qk):
            x = qkv[:, c * LANES:(c + 1) * LANES]
            is_q = c < GQA_Q_HEADS
            x = _rms(x) * (qn_ref[...] if is_q else kn_ref[...])
            x = x * cos + _swap_rope_halves(x, first_half) * sin
            if is_q:
                q_out[:, c * LANES:(c + 1) * LANES] = (x * q_scale).astype(BF16)
            else:
                cc = c - GQA_Q_HEADS
                k_out[:, cc * LANES:(cc + 1) * LANES] = x.astype(BF16)
        v_out[...] = qkv[:, n_qk * LANES:].astype(BF16)


def _rope_tables(seq):
    t = jnp.arange(seq)
    pos = jnp.stack([(t // GRID_W).astype(F32), (t % GRID_W).astype(F32)], axis=1)
    axis_dim = GQA_HEAD_DIM // 2
    inv_freq = ROPE_THETA ** (-jnp.arange(0, axis_dim, 2, dtype=F32) / axis_dim)
    ang = pos[:, :, None] * inv_freq
    cos = jnp.cos(ang)
    sin = jnp.sin(ang)
    cos_t = jnp.concatenate([cos[:, 0], cos[:, 0], cos[:, 1], cos[:, 1]], axis=-1)
    sin_t = jnp.concatenate([-sin[:, 0], sin[:, 0], -sin[:, 1], sin[:, 1]], axis=-1)
    return cos_t, sin_t


def _post(o, h, p, w, *, o_is_paired, gqa=None, prompt_tokens=0, prompt_seq=0, tm=256):
    t = h.shape[0]
    row = lambda width: pl.BlockSpec((tm, width), lambda i: (i, 0))
    o_spec = (pl.BlockSpec((NA_PAIRS, tm, LANES), lambda i: (0, i, 0)) if o_is_paired else row(D_MODEL))
    in_specs = [o_spec, row(D_MODEL), row(PLE_DIM),
                _resident((D_MODEL, D_MODEL)), _resident((1, D_MODEL)), _resident((1, D_MODEL)),
                _resident((D_MODEL, 2 * D_FF)), _resident((D_FF, D_MODEL)), _resident((1, D_MODEL)),
                _resident((D_MODEL, D_MODEL)), _resident((PLE_DIM, D_MODEL))]
    args = [o, h, p, w["wo"], w["g_post"], w["g_ffn_pre"], w["w_gu"], w["w_down"], w["g_ffn_post"],
            w["w_ple_gate"], w["w_ple_proj"]]
    out_specs = [row(D_MODEL)]
    out_shape = [jax.ShapeDtypeStruct((t, D_MODEL), F32)]
    if gqa is not None:
        kv_width = GQA_KV_HEADS * GQA_HEAD_DIM
        prompt_blocks = prompt_tokens // tm
        per_seq = prompt_seq // tm

        def pos_block(i):
            return (jnp.where(i < prompt_blocks, i % per_seq, i - prompt_blocks), 0)

        in_specs += [_resident((1, D_MODEL)), _resident((D_MODEL, D_MODEL + 2 * kv_width)),
                     _resident((1, LANES)), _resident((1, LANES)),
                     pl.BlockSpec((tm, LANES), pos_block), pl.BlockSpec((tm, LANES), pos_block)]
        args += [gqa["g_pre"], gqa["w_qkv"], gqa["q_norm"], gqa["k_norm"], gqa["cos"], gqa["sin"]]
        out_specs += [row(D_MODEL), row(kv_width), row(kv_width)]
        out_shape += [jax.ShapeDtypeStruct((t, D_MODEL), BF16),
                      jax.ShapeDtypeStruct((t, kv_width), BF16),
                      jax.ShapeDtypeStruct((t, kv_width), BF16)]
    return pl.pallas_call(
        functools.partial(_post_kernel, o_is_paired=o_is_paired, with_gqa=gqa is not None),
        grid=(t // tm,),
        in_specs=in_specs,
        out_specs=out_specs,
        out_shape=out_shape,
        compiler_params=pltpu.CompilerParams(
            dimension_semantics=("parallel",), vmem_limit_bytes=VMEM_LIMIT),
        name="post_gqa" if gqa is not None else "post_final",
    )(*args)


def _flash_kernel(q_ref, k_ref, v_ref, o_ref, *, tk):
    tq = q_ref.shape[0]
    q = q_ref[...]
    q2 = jnp.concatenate([q[:, :LANES], q[:, LANES:]], axis=0)
    n_k = k_ref.shape[0] // tk

    def body(j, carry):
        m, l, acc = carry
        start = pl.multiple_of(j * tk, tk)
        s = _dot_nt(q2, k_ref[pl.ds(start, tk), :])
        m_new = jnp.maximum(m, jnp.max(s, axis=-1, keepdims=True))
        alpha = jnp.exp2(m - m_new)
        p = jnp.exp2(s - m_new)
        l = alpha * l + jnp.sum(p, axis=-1, keepdims=True)
        acc = alpha * acc + _dot(p.astype(BF16), v_ref[pl.ds(start, tk), :])
        return m_new, l, acc

    init = (jnp.full((2 * tq, 1), NEG_INF, F32), jnp.zeros((2 * tq, 1), F32),
            jnp.zeros((2 * tq, LANES), F32))
    _, l, acc = lax.fori_loop(0, n_k, body, init)
    o2 = acc / l
    o_ref[...] = jnp.concatenate([o2[:tq], o2[tq:]], axis=-1).astype(BF16)


def _flash(q, k, v, *, token_offset, n_seq, seq, tq=128, tk=512):
    t = q.shape[0]
    q_blocks = seq // tq
    q_off = token_offset // tq
    seq_off = token_offset // seq
    qo_spec = pl.BlockSpec((tq, 2 * LANES), lambda b, g, i: (q_off + b * q_blocks + i, g))
    kv_spec = pl.BlockSpec((seq, LANES), lambda b, g, i: (seq_off + b, g))
    return pl.pallas_call(
        functools.partial(_flash_kernel, tk=tk),
        grid=(n_seq, GQA_KV_HEADS, q_blocks),
        in_specs=[qo_spec, kv_spec, kv_spec],
        out_specs=pl.BlockSpec((tq, 2 * LANES), lambda b, g, i: (b * q_blocks + i, g)),
        out_shape=jax.ShapeDtypeStruct((n_seq * seq, D_MODEL), BF16),
        compiler_params=pltpu.CompilerParams(
            dimension_semantics=("parallel", "parallel", "arbitrary"), vmem_limit_bytes=VMEM_LIMIT),
        name=f"flash_{n_seq}x{seq}",
    )(q, k, v)


def kernel(x_prompt, x_sample, p_prompt, p_sample, mix_pre_norm, mix_post_norm, ffn_pre_norm, ffn_post_norm,
           na_w_qkv, na_rpb, na_w_o, gqa_w_qkv, gqa_q_norm, gqa_k_norm, gqa_w_o, ffn_w_gate_up, ffn_w_down,
           ple_w_gate, ple_w_proj):
    n_prompt, prompt_seq, _ = x_prompt.shape
    n_sample, sample_seq, _ = x_sample.shape
    prompt_tokens = n_prompt * prompt_seq
    sample_tokens = n_sample * sample_seq
    assert n_sample == 1 and prompt_tokens % sample_seq == 0

    h = jnp.concatenate([x_prompt.reshape(prompt_tokens, D_MODEL), x_sample.reshape(sample_tokens, D_MODEL)])
    p = jnp.concatenate([p_prompt.reshape(2, prompt_tokens, PLE_DIM),
                         p_sample.reshape(2, sample_tokens, PLE_DIM)], axis=1)

    gain = lambda g: g.reshape(1, -1)
    layer_w = [dict(wo=(na_w_o[0] if i == 0 else gqa_w_o[0]).astype(BF16),
                    g_post=gain(mix_post_norm[i]), g_ffn_pre=gain(ffn_pre_norm[i]),
                    w_gu=ffn_w_gate_up[i].astype(BF16), w_down=ffn_w_down[i].astype(BF16),
                    g_ffn_post=gain(ffn_post_norm[i]),
                    w_ple_gate=ple_w_gate[i].astype(BF16), w_ple_proj=ple_w_proj[i].astype(BF16))
               for i in range(2)]
    cos, sin = _rope_tables(sample_seq)
    gqa = dict(g_pre=gain(mix_pre_norm[1]), w_qkv=gqa_w_qkv[0].astype(BF16),
               q_norm=gain(gqa_q_norm[0]), k_norm=gain(gqa_k_norm[0]), cos=cos, sin=sin)

    qkv = _pre_na(h, gain(mix_pre_norm[0]), na_w_qkv[0].astype(BF16))
    o = _na_attention(qkv, _na_bias_table(na_rpb[0]), prompt_tokens, prompt_seq, sample_seq)
    h, q, k, v = _post(o, h, p[0], layer_w[0], o_is_paired=True, gqa=gqa,
                       prompt_tokens=prompt_tokens, prompt_seq=prompt_seq)

    o = jnp.concatenate([
        _flash(q, k, v, token_offset=0, n_seq=n_prompt, seq=prompt_seq),
        _flash(q, k, v, token_offset=prompt_tokens, n_seq=n_sample, seq=sample_seq)])
    (h,) = _post(o, h, p[1], layer_w[1], o_is_paired=False)

    return (h[:prompt_tokens].reshape(x_prompt.shape), h[prompt_tokens:].reshape(x_sample.shape))
```

```python
import functools
import math

import jax
import jax.numpy as jnp
import numpy as np
from jax import lax
from jax.experimental import pallas as pl
from jax.experimental.pallas import tpu as pltpu

F32 = jnp.float32
BF16 = jnp.bfloat16

D_MODEL = 1024
GRID_W = 64
NA_HEADS = 16
NA_HEAD_DIM = 64
NA_WIN_ROWS = 8
NA_WIN_COLS = 16
GQA_Q_HEADS = 8
GQA_KV_HEADS = 4
GQA_HEAD_DIM = 128
ROPE_THETA = 10000.0
PLE_DIM = 256
D_FF = 2816
EPS = 1e-6
NEG_INF = -1e30
LOG2E = math.log2(math.e)

LANES = 128
VMEM_LIMIT = 56 * 1024 * 1024

NA_Q_ROWS = 4
NA_K_ROWS = 12
NA_TQ = NA_Q_ROWS * GRID_W
NA_TK = NA_K_ROWS * GRID_W
NA_PAIRS = NA_HEADS // 2
FLASH_TK = 512


def _rms(x):
    return x * lax.rsqrt(jnp.mean(x * x, axis=-1, keepdims=True) + EPS)


def _sigmoid(x):
    return 1.0 / (1.0 + jnp.exp(-x))


def _dot(a, b):
    return jnp.dot(a, b, preferred_element_type=F32)


def _dot_nt(a, b):
    return lax.dot_general(a, b, (((1,), (1,)), ((), ())), preferred_element_type=F32)


def _resident(shape):
    return pl.BlockSpec(shape, lambda *_: (0,) * len(shape), pipeline_mode=pl.Buffered(1))


def _pre_na_kernel(x_ref, g_ref, w_ref, o_ref):
    a = (_rms(x_ref[...]) * g_ref[...]).astype(BF16)
    qkv = _dot(a, w_ref[...])
    n_q = NA_PAIRS
    q_scale = NA_HEAD_DIM ** -0.5 * LOG2E
    for c in range(3 * NA_PAIRS):
        piece = qkv[:, c * LANES:(c + 1) * LANES]
        if c < n_q:
            piece = piece * q_scale
        o_ref[c] = piece.astype(BF16)


def _pre_na(x, gain, w_qkv, tm=512):
    t = x.shape[0]
    return pl.pallas_call(
        _pre_na_kernel,
        grid=(t // tm,),
        in_specs=[
            pl.BlockSpec((tm, D_MODEL), lambda i: (i, 0)),
            _resident((1, D_MODEL)),
            _resident((D_MODEL, 3 * D_MODEL)),
        ],
        out_specs=pl.BlockSpec((3 * NA_PAIRS, tm, LANES), lambda i: (0, i, 0)),
        out_shape=jax.ShapeDtypeStruct((3 * NA_PAIRS, t, LANES), BF16),
        compiler_params=pltpu.CompilerParams(
            dimension_semantics=("parallel",), vmem_limit_bytes=VMEM_LIMIT),
        name="pre_na",
    )(x, gain, w_qkv)


def _na_bias_table(rpb):
    qc = np.arange(GRID_W)[:, None]
    kc = np.arange(GRID_W)[None, :]
    wstart = np.clip(qc - NA_WIN_COLS // 2, 0, GRID_W - NA_WIN_COLS)
    col_ok = (kc >= wstart) & (kc < wstart + NA_WIN_COLS)
    col_off = np.clip(kc - qc + NA_WIN_COLS - 1, 0, 2 * NA_WIN_COLS - 2)
    n_dc = 2 * NA_WIN_COLS - 1
    n_dr = 2 * NA_WIN_ROWS - 1
    onehot = (col_off[None] == np.arange(n_dc)[:, None, None]).astype(np.float32)
    cols = jnp.einsum("hdc,cqk->hdqk", rpb, jnp.asarray(onehot), precision=lax.Precision.HIGHEST)
    cols = jnp.where(jnp.asarray(col_ok), cols * LOG2E, NEG_INF)
    cols = jnp.concatenate([cols, jnp.full((NA_HEADS, 1, GRID_W, GRID_W), NEG_INF, F32)], axis=1)
    rq = np.arange(NA_Q_ROWS)[:, None]
    u = np.arange(NA_K_ROWS)[None, :]
    tables = []
    for variant in range(3):
        if variant == 0:
            dr = u - rq + NA_WIN_ROWS - 1
            row_ok = np.broadcast_to(u < NA_WIN_ROWS, dr.shape)
        elif variant == 1:
            dr = u - rq + NA_WIN_ROWS - 1 - NA_Q_ROWS
            row_ok = (u >= rq) & (u < rq + NA_WIN_ROWS)
        else:
            dr = u - rq + NA_WIN_ROWS - 1 - 2 * NA_Q_ROWS
            row_ok = np.broadcast_to(u >= NA_K_ROWS - NA_WIN_ROWS, dr.shape)
        dr = np.where(row_ok, dr, n_dr)
        assert dr.min() >= 0 and dr.max() <= n_dr
        vals = jnp.take(cols, jnp.asarray(dr.reshape(-1)), axis=1)
        vals = vals.reshape(NA_HEADS, NA_Q_ROWS, NA_K_ROWS, GRID_W, GRID_W).transpose(0, 1, 3, 2, 4)
        tables.append(vals.reshape(NA_PAIRS, 2 * NA_TQ, NA_TK))
    return jnp.stack(tables)


def _na_kernel(q_ref, k0_ref, k1_ref, k2_ref, v0_ref, v1_ref, v2_ref, bias_ref, o_ref):
    lane = lax.broadcasted_iota(jnp.int32, (1, LANES), 1)
    first_head = lane < NA_HEAD_DIM
    k_refs = (k0_ref, k1_ref, k2_ref)
    v_refs = (v0_ref, v1_ref, v2_ref)
    n_blk = len(k_refs)
    tkb = NA_TK // n_blk

    def pair_body(hp, carry):
        q = q_ref[hp]
        zero = jnp.zeros_like(q)
        q2 = jnp.concatenate([jnp.where(first_head, q, zero), jnp.where(first_head, zero, q)], axis=0)
        s = [_dot_nt(q2, k_refs[j][hp]) + bias_ref[0, hp, :, j * tkb:(j + 1) * tkb] for j in range(n_blk)]
        m = functools.reduce(jnp.maximum, [jnp.max(sj, axis=-1, keepdims=True) for sj in s])
        p = [jnp.exp2(sj - m) for sj in s]
        l = functools.reduce(jnp.add, [jnp.sum(pj, axis=-1, keepdims=True) for pj in p])
        o2 = functools.reduce(jnp.add, [_dot(p[j].astype(BF16), v_refs[j][hp]) for j in range(n_blk)])
        o2 = o2 / l
        o_ref[hp] = jnp.where(first_head, o2[:NA_TQ], o2[NA_TQ:]).astype(BF16)
        return carry

    lax.fori_loop(0, NA_PAIRS, pair_body, 0)


def _na_geometry(i, n_prompt_blocks, prompt_blocks_per_seq, sample_blocks):
    is_prompt = i < n_prompt_blocks
    n_blk = jnp.where(is_prompt, prompt_blocks_per_seq, sample_blocks)
    first = jnp.where(is_prompt, (i // prompt_blocks_per_seq) * prompt_blocks_per_seq, n_prompt_blocks)
    local = i - first
    win = first + jnp.clip(local - 1, 0, n_blk - 3)
    variant = jnp.where(local == 0, 0, jnp.where(local == n_blk - 1, 2, 1))
    return win, variant


def _na_attention(qkv, bias, prompt_tokens, prompt_seq, sample_seq):
    t = qkv.shape[1]
    geom = functools.partial(
        _na_geometry,
        n_prompt_blocks=prompt_tokens // NA_TQ,
        prompt_blocks_per_seq=prompt_seq // NA_TQ,
        sample_blocks=sample_seq // NA_TQ)
    assert prompt_seq // NA_TQ >= 3 and sample_seq // NA_TQ >= 3

    def kv_spec(part, j):
        return pl.BlockSpec((NA_PAIRS, NA_TQ, LANES), lambda i: (part, geom(i)[0] + j, 0))

    return pl.pallas_call(
        _na_kernel,
        grid=(t // NA_TQ,),
        in_specs=[pl.BlockSpec((NA_PAIRS, NA_TQ, LANES), lambda i: (0, i, 0))]
        + [kv_spec(1, j) for j in range(3)] + [kv_spec(2, j) for j in range(3)]
        + [pl.BlockSpec((1, NA_PAIRS, 2 * NA_TQ, NA_TK), lambda i: (geom(i)[1], 0, 0, 0))],
        out_specs=pl.BlockSpec((NA_PAIRS, NA_TQ, LANES), lambda i: (0, i, 0)),
        out_shape=jax.ShapeDtypeStruct((NA_PAIRS, t, LANES), BF16),
        compiler_params=pltpu.CompilerParams(
            dimension_semantics=("parallel",), vmem_limit_bytes=VMEM_LIMIT),
        name="na_attention",
    )(qkv, qkv, qkv, qkv, qkv, qkv, qkv, bias)


def _swap_rope_halves(x, lane_in_first_half):
    return jnp.where(lane_in_first_half, pltpu.roll(x, 96, 1), pltpu.roll(x, 32, 1))


def _post_kernel(*refs, o_is_paired, with_gqa):
    if with_gqa:
        (o_ref, h_ref, p_ref, wo_ref, gpost_ref, gfpre_ref, wgu_ref, wd_ref, gfpost_ref, wpg_ref, wpp_ref,
         gnext_ref, wqkv_ref, qn_ref, kn_ref, cos_ref, sin_ref, h_out, q_out, k_out, v_out) = refs
    else:
        (o_ref, h_ref, p_ref, wo_ref, gpost_ref, gfpre_ref, wgu_ref, wd_ref, gfpost_ref, wpg_ref, wpp_ref,
         h_out) = refs

    if o_is_paired:
        o = jnp.concatenate([o_ref[c] for c in range(NA_PAIRS)], axis=-1)
    else:
        o = o_ref[...]
    h = h_ref[...] + _rms(_dot(o, wo_ref[...])) * gpost_ref[...]

    a = (_rms(h) * gfpre_ref[...]).astype(BF16)
    g = _dot(a, wgu_ref[:, :D_FF])
    u = _dot(a, wgu_ref[:, D_FF:])
    act = (g * _sigmoid(g) * u).astype(BF16)
    h = h + _rms(_dot(act, wd_ref[...])) * gfpost_ref[...]

    gate = _sigmoid(_dot(_rms(h).astype(BF16), wpg_ref[...]))
    h = h + _dot(p_ref[...].astype(BF16), wpp_ref[...]) * gate
    h_out[...] = h

    if with_gqa:
        a = (_rms(h) * gnext_ref[...]).astype(BF16)
        qkv = _dot(a, wqkv_ref[...])
        cos = cos_ref[...]
        sin = sin_ref[...]
        lane = lax.broadcasted_iota(jnp.int32, (1, LANES), 1)
        first_half = (lane % 64) < 32
        q_scale = GQA_HEAD_DIM ** -0.5 * LOG2E
        n_qk = GQA_Q_HEADS + GQA_KV_HEADS
        for c in range(n_qk):
            x = qkv[:, c * LANES:(c + 1) * LANES]
            is_q = c < GQA_Q_HEADS
            x = _rms(x) * (qn_ref[...] if is_q else kn_ref[...])
            x = x * cos + _swap_rope_halves(x, first_half) * sin
            if is_q:
                q_out[:, c * LANES:(c + 1) * LANES] = (x * q_scale).astype(BF16)
            else:
                cc = c - GQA_Q_HEADS
                k_out[:, cc * LANES:(cc + 1) * LANES] = x.astype(BF16)
        v_out[...] = qkv[:, n_qk * LANES:].astype(BF16)


def _rope_tables(seq):
    t = jnp.arange(seq)
    pos = jnp.stack([(t // GRID_W).astype(F32), (t % GRID_W).astype(F32)], axis=1)
    axis_dim = GQA_HEAD_DIM // 2
    inv_freq = ROPE_THETA ** (-jnp.arange(0, axis_dim, 2, dtype=F32) / axis_dim)
    ang = pos[:, :, None] * inv_freq
    cos = jnp.cos(ang)
    sin = jnp.sin(ang)
    cos_t = jnp.concatenate([cos[:, 0], cos[:, 0], cos[:, 1], cos[:, 1]], axis=-1)
    sin_t = jnp.concatenate([-sin[:, 0], sin[:, 0], -sin[:, 1], sin[:, 1]], axis=-1)
    return cos_t, sin_t


def _post(o, h, p, w, *, o_is_paired, gqa=None, prompt_tokens=0, prompt_seq=0, tm=256):
    t = h.shape[0]
    row = lambda width: pl.BlockSpec((tm, width), lambda i: (i, 0))
    o_spec = (pl.BlockSpec((NA_PAIRS, tm, LANES), lambda i: (0, i, 0)) if o_is_paired else row(D_MODEL))
    in_specs = [o_spec, row(D_MODEL), row(PLE_DIM),
                _resident((D_MODEL, D_MODEL)), _resident((1, D_MODEL)), _resident((1, D_MODEL)),
                _resident((D_MODEL, 2 * D_FF)), _resident((D_FF, D_MODEL)), _resident((1, D_MODEL)),
                _resident((D_MODEL, D_MODEL)), _resident((PLE_DIM, D_MODEL))]
    args = [o, h, p, w["wo"], w["g_post"], w["g_ffn_pre"], w["w_gu"], w["w_down"], w["g_ffn_post"],
            w["w_ple_gate"], w["w_ple_proj"]]
    out_specs = [row(D_MODEL)]
    out_shape = [jax.ShapeDtypeStruct((t, D_MODEL), F32)]
    if gqa is not None:
        kv_width = GQA_KV_HEADS * GQA_HEAD_DIM
        prompt_blocks = prompt_tokens // tm
        per_seq = prompt_seq // tm

        def pos_block(i):
            return (jnp.where(i < prompt_blocks, i % per_seq, i - prompt_blocks), 0)

        in_specs += [_resident((1, D_MODEL)), _resident((D_MODEL, D_MODEL + 2 * kv_width)),
                     _resident((1, LANES)), _resident((1, LANES)),
                     pl.BlockSpec((tm, LANES), pos_block), pl.BlockSpec((tm, LANES), pos_block)]
        args += [gqa["g_pre"], gqa["w_qkv"], gqa["q_norm"], gqa["k_norm"], gqa["cos"], gqa["sin"]]
        out_specs += [row(D_MODEL), row(kv_width), row(kv_width)]
        out_shape += [jax.ShapeDtypeStruct((t, D_MODEL), BF16),
                      jax.ShapeDtypeStruct((t, kv_width), BF16),
                      jax.ShapeDtypeStruct((t, kv_width), BF16)]
    return pl.pallas_call(
        functools.partial(_post_kernel, o_is_paired=o_is_paired, with_gqa=gqa is not None),
        grid=(t // tm,),
        in_specs=in_specs,
        out_specs=out_specs,
        out_shape=out_shape,
        compiler_params=pltpu.CompilerParams(
            dimension_semantics=("parallel",), vmem_limit_bytes=VMEM_LIMIT),
        name="post_gqa" if gqa is not None else "post_final",
    )(*args)


def _flash_kernel(q_ref, k_ref, vt_ref, o_ref, s0_ref, s1_ref, p0_ref, p1_ref):
    tq = q_ref.shape[0]
    n_k, _, tk = vt_ref.shape[1:]
    q = q_ref[...].astype(F32)
    q2t = jnp.concatenate([q[:, :LANES].T, q[:, LANES:].T], axis=1).astype(BF16)
    s_bufs = (s0_ref, s1_ref)
    p_bufs = (p0_ref, p1_ref)

    def scores(j, slot):
        start = pl.multiple_of(j * tk, tk)
        s_bufs[slot][...] = _dot(k_ref[pl.ds(start, tk), :], q2t)

    def softmax(slot, m, l):
        st = s_bufs[slot][...]
        m_new = jnp.maximum(m, jnp.max(st, axis=0, keepdims=True))
        alpha = jnp.exp2(m - m_new)
        pt = jnp.exp2(st - m_new)
        p_bufs[slot][...] = pt.astype(BF16)
        return m_new, alpha * l + jnp.sum(pt, axis=0, keepdims=True), alpha

    def values(j, slot, alpha, acc):
        return alpha * acc + _dot(vt_ref[0, j], p_bufs[slot][...])

    m = jnp.full((1, 2 * tq), NEG_INF, F32)
    l = jnp.zeros((1, 2 * tq), F32)
    acc = jnp.zeros((LANES, 2 * tq), F32)
    scores(0, 0)
    scores(1, 1)
    m, l, alpha = softmax(0, m, l)

    def two_steps(jj, carry):
        m, l, alpha_prev, acc = carry
        j = 2 * jj + 1
        scores(j + 1, 0)
        m, l, alpha = softmax(1, m, l)
        acc = values(j - 1, 0, alpha_prev, acc)
        scores(j + 2, 1)
        m, l, alpha_next = softmax(0, m, l)
        acc = values(j, 1, alpha, acc)
        return m, l, alpha_next, acc

    assert n_k >= 2 and n_k % 2 == 0
    m, l, alpha_prev, acc = lax.fori_loop(0, (n_k - 2) // 2, two_steps, (m, l, alpha, acc))
    m, l, alpha = softmax(1, m, l)
    acc = values(n_k - 2, 0, alpha_prev, acc)
    acc = values(n_k - 1, 1, alpha, acc)
    ot = acc / l
    o_ref[...] = jnp.concatenate([ot[:, :tq].T, ot[:, tq:].T], axis=1).astype(BF16)


def _flash(q, k, vt, *, token_offset, n_seq, seq, tq=256):
    tk = vt.shape[-1]
    q_blocks = seq // tq
    q_off = token_offset // tq
    seq_off = token_offset // seq
    return pl.pallas_call(
        _flash_kernel,
        grid=(n_seq, GQA_KV_HEADS, q_blocks),
        in_specs=[pl.BlockSpec((tq, 2 * LANES), lambda b, g, i: (q_off + b * q_blocks + i, g)),
                  pl.BlockSpec((seq, LANES), lambda b, g, i: (seq_off + b, g)),
                  pl.BlockSpec((1, seq // tk, LANES, tk), lambda b, g, i: (g, seq_off + b, 0, 0))],
        out_specs=pl.BlockSpec((tq, 2 * LANES), lambda b, g, i: (b * q_blocks + i, g)),
        out_shape=jax.ShapeDtypeStruct((n_seq * seq, D_MODEL), BF16),
        scratch_shapes=[pltpu.VMEM((tk, 2 * tq), F32), pltpu.VMEM((tk, 2 * tq), F32),
                        pltpu.VMEM((tk, 2 * tq), BF16), pltpu.VMEM((tk, 2 * tq), BF16)],
        compiler_params=pltpu.CompilerParams(
            dimension_semantics=("parallel", "parallel", "arbitrary"), vmem_limit_bytes=VMEM_LIMIT),
        name=f"flash_{n_seq}x{seq}",
    )(q, k, vt)


def kernel(x_prompt, x_sample, p_prompt, p_sample, mix_pre_norm, mix_post_norm, ffn_pre_norm, ffn_post_norm,
           na_w_qkv, na_rpb, na_w_o, gqa_w_qkv, gqa_q_norm, gqa_k_norm, gqa_w_o, ffn_w_gate_up, ffn_w_down,
           ple_w_gate, ple_w_proj):
    n_prompt, prompt_seq, _ = x_prompt.shape
    n_sample, sample_seq, _ = x_sample.shape
    prompt_tokens = n_prompt * prompt_seq
    sample_tokens = n_sample * sample_seq
    assert n_sample == 1 and prompt_tokens % sample_seq == 0

    h = jnp.concatenate([x_prompt.reshape(prompt_tokens, D_MODEL), x_sample.reshape(sample_tokens, D_MODEL)])
    p = jnp.concatenate([p_prompt.reshape(2, prompt_tokens, PLE_DIM),
                         p_sample.reshape(2, sample_tokens, PLE_DIM)], axis=1)

    gain = lambda g: g.reshape(1, -1)
    layer_w = [dict(wo=(na_w_o[0] if i == 0 else gqa_w_o[0]).astype(BF16),
                    g_post=gain(mix_post_norm[i]), g_ffn_pre=gain(ffn_pre_norm[i]),
                    w_gu=ffn_w_gate_up[i].astype(BF16), w_down=ffn_w_down[i].astype(BF16),
                    g_ffn_post=gain(ffn_post_norm[i]),
                    w_ple_gate=ple_w_gate[i].astype(BF16), w_ple_proj=ple_w_proj[i].astype(BF16))
               for i in range(2)]
    cos, sin = _rope_tables(sample_seq)
    gqa = dict(g_pre=gain(mix_pre_norm[1]), w_qkv=gqa_w_qkv[0].astype(BF16),
               q_norm=gain(gqa_q_norm[0]), k_norm=gain(gqa_k_norm[0]), cos=cos, sin=sin)

    qkv = _pre_na(h, gain(mix_pre_norm[0]), na_w_qkv[0].astype(BF16))
    o = _na_attention(qkv, _na_bias_table(na_rpb[0]), prompt_tokens, prompt_seq, sample_seq)
    h, q, k, v = _post(o, h, p[0], layer_w[0], o_is_paired=True, gqa=gqa,
                       prompt_tokens=prompt_tokens, prompt_seq=prompt_seq)

    t = prompt_tokens + sample_tokens
    vt = v.reshape(t // FLASH_TK, FLASH_TK, GQA_KV_HEADS, GQA_HEAD_DIM).transpose(2, 0, 3, 1)
    o = jnp.concatenate([
        _flash(q, k, vt, token_offset=0, n_seq=n_prompt, seq=prompt_seq),
        _flash(q, k, vt, token_offset=prompt_tokens, n_seq=n_sample, seq=sample_seq)])
    (h,) = _post(o, h, p[1], layer_w[1], o_is_paired=False)

    return (h[:prompt_tokens].reshape(x_prompt.shape), h[prompt_tokens:].reshape(x_sample.shape))
```

```python
import functools
import math

import jax
import jax.numpy as jnp
import numpy as np
from jax import lax
from jax.experimental import pallas as pl
from jax.experimental.pallas import tpu as pltpu

F32 = jnp.float32
BF16 = jnp.bfloat16

D_MODEL = 1024
GRID_W = 64
NA_HEADS = 16
NA_HEAD_DIM = 64
NA_WIN_ROWS = 8
NA_WIN_COLS = 16
GQA_Q_HEADS = 8
GQA_KV_HEADS = 4
GQA_HEAD_DIM = 128
ROPE_THETA = 10000.0
PLE_DIM = 256
D_FF = 2816
EPS = 1e-6
NEG_INF = -1e30
LOG2E = math.log2(math.e)

LANES = 128
VMEM_LIMIT = 56 * 1024 * 1024

NA_Q_ROWS = 4
NA_K_ROWS = 12
NA_TQ = NA_Q_ROWS * GRID_W
NA_TK = NA_K_ROWS * GRID_W
NA_PAIRS = NA_HEADS // 2
FLASH_TK = 512
FLASH_P_BUFFERS = 3
FLASH_NORM_SLACK = 1.01
FLASH_SAFE_SHIFT = 40.0


def _rms(x):
    return x * lax.rsqrt(jnp.mean(x * x, axis=-1, keepdims=True) + EPS)


def _sigmoid(x):
    return 1.0 / (1.0 + jnp.exp(-x))


def _dot(a, b):
    return jnp.dot(a, b, preferred_element_type=F32)


def _dot_nt(a, b):
    return lax.dot_general(a, b, (((1,), (1,)), ((), ())), preferred_element_type=F32)


def _resident(shape):
    return pl.BlockSpec(shape, lambda *_: (0,) * len(shape), pipeline_mode=pl.Buffered(1))


def _pre_na_kernel(x_ref, g_ref, w_ref, o_ref):
    a = (_rms(x_ref[...]) * g_ref[...]).astype(BF16)
    qkv = _dot(a, w_ref[...])
    n_q = NA_PAIRS
    q_scale = NA_HEAD_DIM ** -0.5 * LOG2E
    for c in range(3 * NA_PAIRS):
        piece = qkv[:, c * LANES:(c + 1) * LANES]
        if c < n_q:
            piece = piece * q_scale
        o_ref[c] = piece.astype(BF16)


def _pre_na(x, gain, w_qkv, tm=512):
    t = x.shape[0]
    return pl.pallas_call(
        _pre_na_kernel,
        grid=(t // tm,),
        in_specs=[
            pl.BlockSpec((tm, D_MODEL), lambda i: (i, 0)),
            _resident((1, D_MODEL)),
            _resident((D_MODEL, 3 * D_MODEL)),
        ],
        out_specs=pl.BlockSpec((3 * NA_PAIRS, tm, LANES), lambda i: (0, i, 0)),
        out_shape=jax.ShapeDtypeStruct((3 * NA_PAIRS, t, LANES), BF16),
        compiler_params=pltpu.CompilerParams(
            dimension_semantics=("parallel",), vmem_limit_bytes=VMEM_LIMIT),
        name=f"pre_na_{t}",
    )(x, gain, w_qkv)


def _na_bias_table(rpb):
    qc = np.arange(GRID_W)[:, None]
    kc = np.arange(GRID_W)[None, :]
    wstart = np.clip(qc - NA_WIN_COLS // 2, 0, GRID_W - NA_WIN_COLS)
    col_ok = (kc >= wstart) & (kc < wstart + NA_WIN_COLS)
    col_off = np.clip(kc - qc + NA_WIN_COLS - 1, 0, 2 * NA_WIN_COLS - 2)
    n_dc = 2 * NA_WIN_COLS - 1
    n_dr = 2 * NA_WIN_ROWS - 1
    onehot = (col_off[None] == np.arange(n_dc)[:, None, None]).astype(np.float32)
    cols = jnp.einsum("hdc,cqk->hdqk", rpb, jnp.asarray(onehot), precision=lax.Precision.HIGHEST)
    cols = jnp.where(jnp.asarray(col_ok), cols * LOG2E, NEG_INF)
    cols = jnp.concatenate([cols, jnp.full((NA_HEADS, 1, GRID_W, GRID_W), NEG_INF, F32)], axis=1)
    rq = np.arange(NA_Q_ROWS)[:, None]
    u = np.arange(NA_K_ROWS)[None, :]
    tables = []
    for variant in range(3):
        if variant == 0:
            dr = u - rq + NA_WIN_ROWS - 1
            row_ok = np.broadcast_to(u < NA_WIN_ROWS, dr.shape)
        elif variant == 1:
            dr = u - rq + NA_WIN_ROWS - 1 - NA_Q_ROWS
            row_ok = (u >= rq) & (u < rq + NA_WIN_ROWS)
        else:
            dr = u - rq + NA_WIN_ROWS - 1 - 2 * NA_Q_ROWS
            row_ok = np.broadcast_to(u >= NA_K_ROWS - NA_WIN_ROWS, dr.shape)
        dr = np.where(row_ok, dr, n_dr)
        assert dr.min() >= 0 and dr.max() <= n_dr
        vals = jnp.take(cols, jnp.asarray(dr.reshape(-1)), axis=1)
        vals = vals.reshape(NA_PAIRS, 2, NA_Q_ROWS, NA_K_ROWS, GRID_W, GRID_W).transpose(0, 3, 5, 1, 2, 4)
        tables.append(vals.reshape(NA_PAIRS, NA_TK, 2 * NA_TQ))
    return jnp.stack(tables)


def _na_kernel(q_ref, k0_ref, k1_ref, k2_ref, vt0_ref, vt1_ref, vt2_ref, bias_ref, o_ref,
               s0_ref, s1_ref, p0_ref, p1_ref):
    first_head_lane = lax.broadcasted_iota(jnp.int32, (1, LANES), 1) < NA_HEAD_DIM
    first_head_row = lax.broadcasted_iota(jnp.int32, (LANES, 1), 0) < NA_HEAD_DIM
    k_refs = (k0_ref, k1_ref, k2_ref)
    vt_refs = (vt0_ref, vt1_ref, vt2_ref)
    s_bufs = (s0_ref, s1_ref)
    p_bufs = (p0_ref, p1_ref)
    n_blk = len(k_refs)
    tkb = NA_TK // n_blk

    def scores(hp, slot):
        q = q_ref[hp]
        zero = jnp.zeros_like(q)
        q2 = jnp.concatenate([jnp.where(first_head_lane, q, zero), jnp.where(first_head_lane, zero, q)], axis=0)
        m = None
        for j in range(n_blk):
            rows = slice(j * tkb, (j + 1) * tkb)
            s = _dot_nt(k_refs[j][hp], q2) + bias_ref[0, hp, rows, :]
            s_bufs[slot][rows, :] = s
            mj = jnp.max(s, axis=0, keepdims=True)
            m = mj if m is None else jnp.maximum(m, mj)
        return m

    def softmax(slot, m):
        pt = jnp.exp2(s_bufs[slot][...] - m)
        p_bufs[slot][...] = pt.astype(BF16)
        return jnp.sum(pt, axis=0, keepdims=True)

    def values(hp, slot, l):
        ot = functools.reduce(jnp.add, [
            _dot(vt_refs[j][hp], p_bufs[slot][j * tkb:(j + 1) * tkb, :]) for j in range(n_blk)])
        ot = ot / l
        pair_t = jnp.where(first_head_row, ot[:, :NA_TQ], ot[:, NA_TQ:])
        o_ref[hp] = pair_t.T.astype(BF16)

    m = {}
    l = {}
    for step in range(NA_PAIRS + 2):
        if step < NA_PAIRS:
            m[step] = scores(step, step % 2)
        if 1 <= step <= NA_PAIRS:
            l[step - 1] = softmax((step - 1) % 2, m.pop(step - 1))
        if step >= 2:
            values(step - 2, step % 2, l.pop(step - 2))


def _na_geometry(i, n_blk):
    local = i % n_blk
    win = i - local + jnp.clip(local - 1, 0, n_blk - 3)
    variant = jnp.where(local == 0, 0, jnp.where(local == n_blk - 1, 2, 1))
    return win, variant


def _na_attention(qkv, bias, seq):
    t = qkv.shape[1]
    assert seq // NA_TQ >= 3
    geom = functools.partial(_na_geometry, n_blk=seq // NA_TQ)
    n_win = NA_TK // NA_TQ
    vt = qkv[2 * NA_PAIRS:].transpose(0, 2, 1)

    def k_spec(j):
        return pl.BlockSpec((NA_PAIRS, NA_TQ, LANES), lambda i: (1, geom(i)[0] + j, 0))

    def vt_spec(j):
        return pl.BlockSpec((NA_PAIRS, LANES, NA_TQ), lambda i: (0, 0, geom(i)[0] + j))

    return pl.pallas_call(
        _na_kernel,
        grid=(t // NA_TQ,),
        in_specs=[pl.BlockSpec((NA_PAIRS, NA_TQ, LANES), lambda i: (0, i, 0))]
        + [k_spec(j) for j in range(n_win)] + [vt_spec(j) for j in range(n_win)]
        + [pl.BlockSpec((1, NA_PAIRS, NA_TK, 2 * NA_TQ), lambda i: (geom(i)[1], 0, 0, 0))],
        out_specs=pl.BlockSpec((NA_PAIRS, NA_TQ, LANES), lambda i: (0, i, 0)),
        out_shape=jax.ShapeDtypeStruct((NA_PAIRS, t, LANES), BF16),
        scratch_shapes=[pltpu.VMEM((NA_TK, 2 * NA_TQ), F32)] * 2 + [pltpu.VMEM((NA_TK, 2 * NA_TQ), BF16)] * 2,
        compiler_params=pltpu.CompilerParams(
            dimension_semantics=("parallel",), vmem_limit_bytes=VMEM_LIMIT),
        name=f"na_attention_{t // seq}x{seq}",
    )(qkv, *([qkv] * n_win), *([vt] * n_win), bias)


def _swap_rope_halves(x, lane_in_first_half):
    return jnp.where(lane_in_first_half, pltpu.roll(x, 96, 1), pltpu.roll(x, 32, 1))


def _post_kernel(*refs, o_is_paired, with_gqa):
    if with_gqa:
        (o_ref, h_ref, p_ref, wo_ref, gpost_ref, gfpre_ref, wgu_ref, wd_ref, gfpost_ref, wpg_ref, wpp_ref,
         gnext_ref, wqkv_ref, qn_ref, kn_ref, cos_ref, sin_ref, h_out, q_out, k_out, v_out) = refs
    else:
        (o_ref, h_ref, p_ref, wo_ref, gpost_ref, gfpre_ref, wgu_ref, wd_ref, gfpost_ref, wpg_ref, wpp_ref,
         h_out) = refs

    if o_is_paired:
        o = jnp.concatenate([o_ref[c] for c in range(NA_PAIRS)], axis=-1)
    else:
        o = o_ref[...]
    h = h_ref[...] + _rms(_dot(o, wo_ref[...])) * gpost_ref[...]

    a = (_rms(h) * gfpre_ref[...]).astype(BF16)
    g = _dot(a, wgu_ref[:, :D_FF])
    u = _dot(a, wgu_ref[:, D_FF:])
    act = (g * _sigmoid(g) * u).astype(BF16)
    h = h + _rms(_dot(act, wd_ref[...])) * gfpost_ref[...]

    gate = _sigmoid(_dot(_rms(h).astype(BF16), wpg_ref[...]))
    h = h + _dot(p_ref[...].astype(BF16), wpp_ref[...]) * gate
    h_out[...] = h

    if with_gqa:
        a = (_rms(h) * gnext_ref[...]).astype(BF16)
        qkv = _dot(a, wqkv_ref[...])
        cos = cos_ref[...]
        sin = sin_ref[...]
        lane = lax.broadcasted_iota(jnp.int32, (1, LANES), 1)
        first_half = (lane % 64) < 32
        q_scale = GQA_HEAD_DIM ** -0.5 * LOG2E
        n_qk = GQA_Q_HEADS + GQA_KV_HEADS
        for c in range(n_qk):
            x = qkv[:, c * LANES:(c + 1) * LANES]
            is_q = c < GQA_Q_HEADS
            x = _rms(x) * (qn_ref[...] if is_q else kn_ref[...])
            x = x * cos + _swap_rope_halves(x, first_half) * sin
            if is_q:
                q_out[:, c * LANES:(c + 1) * LANES] = (x * q_scale).astype(BF16)
            else:
                cc = c - GQA_Q_HEADS
                k_out[:, cc * LANES:(cc + 1) * LANES] = x.astype(BF16)
        v_out[...] = qkv[:, n_qk * LANES:].astype(BF16)


def _rope_tables(seq):
    rows = seq // GRID_W
    axis_dim = GQA_HEAD_DIM // 2
    inv_freq = ROPE_THETA ** (-jnp.arange(0, axis_dim, 2, dtype=F32) / axis_dim)
    n_f = inv_freq.shape[0]
    ang_r = jnp.arange(rows, dtype=F32)[:, None] * inv_freq
    ang_c = jnp.arange(GRID_W, dtype=F32)[:, None] * inv_freq
    by_row = lambda a: jnp.broadcast_to(a[:, None, :], (rows, GRID_W, n_f))
    by_col = lambda a: jnp.broadcast_to(a[None, :, :], (rows, GRID_W, n_f))
    cos_r, sin_r, cos_c, sin_c = (by_row(jnp.cos(ang_r)), by_row(jnp.sin(ang_r)),
                                  by_col(jnp.cos(ang_c)), by_col(jnp.sin(ang_c)))
    cos_t = jnp.concatenate([cos_r, cos_r, cos_c, cos_c], axis=-1).reshape(seq, GQA_HEAD_DIM)
    sin_t = jnp.concatenate([-sin_r, sin_r, -sin_c, sin_c], axis=-1).reshape(seq, GQA_HEAD_DIM)
    return cos_t, sin_t


def _post(o, h, p, layer, w, *, seq, o_is_paired, gqa=None, tm=256):
    t = h.shape[0]
    row = lambda width: pl.BlockSpec((tm, width), lambda i: (i, 0))
    o_spec = (pl.BlockSpec((NA_PAIRS, tm, LANES), lambda i: (0, i, 0)) if o_is_paired else row(D_MODEL))
    in_specs = [o_spec, row(D_MODEL), pl.BlockSpec((None, tm, PLE_DIM), lambda i: (layer, i, 0)),
                _resident((D_MODEL, D_MODEL)), _resident((1, D_MODEL)), _resident((1, D_MODEL)),
                _resident((D_MODEL, 2 * D_FF)), _resident((D_FF, D_MODEL)), _resident((1, D_MODEL)),
                _resident((D_MODEL, D_MODEL)), _resident((PLE_DIM, D_MODEL))]
    args = [o, h, p, w["wo"], w["g_post"], w["g_ffn_pre"], w["w_gu"], w["w_down"], w["g_ffn_post"],
            w["w_ple_gate"], w["w_ple_proj"]]
    out_specs = [row(D_MODEL)]
    out_shape = [jax.ShapeDtypeStruct((t, D_MODEL), F32)]
    if gqa is not None:
        kv_width = GQA_KV_HEADS * GQA_HEAD_DIM
        per_seq = seq // tm

        def pos_block(i):
            return (i % per_seq, 0)

        in_specs += [_resident((1, D_MODEL)), _resident((D_MODEL, D_MODEL + 2 * kv_width)),
                     _resident((1, LANES)), _resident((1, LANES)),
                     pl.BlockSpec((tm, LANES), pos_block), pl.BlockSpec((tm, LANES), pos_block)]
        args += [gqa["g_pre"], gqa["w_qkv"], gqa["q_norm"], gqa["k_norm"], gqa["cos"], gqa["sin"]]
        out_specs += [row(D_MODEL), row(kv_width), row(kv_width)]
        out_shape += [jax.ShapeDtypeStruct((t, D_MODEL), BF16),
                      jax.ShapeDtypeStruct((t, kv_width), BF16),
                      jax.ShapeDtypeStruct((t, kv_width), BF16)]
    return pl.pallas_call(
        functools.partial(_post_kernel, o_is_paired=o_is_paired, with_gqa=gqa is not None),
        grid=(t // tm,),
        in_specs=in_specs,
        out_specs=out_specs,
        out_shape=out_shape,
        compiler_params=pltpu.CompilerParams(
            dimension_semantics=("parallel",), vmem_limit_bytes=VMEM_LIMIT),
        name=("post_gqa" if gqa is not None else "post_final") + f"_{t // seq}x{seq}",
    )(*args)


def _transposed_queries(q_ref):
    q = q_ref[...].astype(F32)
    return jnp.concatenate([q[:, :LANES].T, q[:, LANES:].T], axis=1)


def _store_heads(o_ref, ot):
    tq = o_ref.shape[0]
    o_ref[...] = jnp.concatenate([ot[:, :tq].T, ot[:, tq:].T], axis=1).astype(BF16)


def _flash_online_kernel(q_ref, k_ref, vt_ref, o_ref, s0_ref, s1_ref, p0_ref, p1_ref):
    tq = q_ref.shape[0]
    n_k, _, tk = vt_ref.shape[1:]
    q2t = _transposed_queries(q_ref).astype(BF16)
    s_bufs = (s0_ref, s1_ref)
    p_bufs = (p0_ref, p1_ref)

    def scores(j, slot):
        start = pl.multiple_of(j * tk, tk)
        s_bufs[slot][...] = _dot(k_ref[pl.ds(start, tk), :], q2t)

    def softmax(slot, m, l):
        st = s_bufs[slot][...]
        m_new = jnp.maximum(m, jnp.max(st, axis=0, keepdims=True))
        alpha = jnp.exp2(m - m_new)
        pt = jnp.exp2(st - m_new)
        p_bufs[slot][...] = pt.astype(BF16)
        return m_new, alpha * l + jnp.sum(pt, axis=0, keepdims=True), alpha

    def values(j, slot, alpha, acc):
        return alpha * acc + _dot(vt_ref[0, j], p_bufs[slot][...])

    m = jnp.full((1, 2 * tq), NEG_INF, F32)
    l = jnp.zeros((1, 2 * tq), F32)
    acc = jnp.zeros((LANES, 2 * tq), F32)
    scores(0, 0)
    scores(1, 1)
    m, l, alpha = softmax(0, m, l)

    def two_steps(jj, carry):
        m, l, alpha_prev, acc = carry
        j = 2 * jj + 1
        scores(j + 1, 0)
        m, l, alpha = softmax(1, m, l)
        acc = values(j - 1, 0, alpha_prev, acc)
        scores(j + 2, 1)
        m, l, alpha_next = softmax(0, m, l)
        acc = values(j, 1, alpha, acc)
        return m, l, alpha_next, acc

    assert n_k >= 2 and n_k % 2 == 0
    m, l, alpha_prev, acc = lax.fori_loop(0, (n_k - 2) // 2, two_steps, (m, l, alpha, acc))
    m, l, alpha = softmax(1, m, l)
    acc = values(n_k - 2, 0, alpha_prev, acc)
    acc = values(n_k - 1, 1, alpha, acc)
    _store_heads(o_ref, acc / l)


def _flash_bounded_kernel(kmax_ref, q_ref, k_ref, vt_ref, o_ref, *p_bufs):
    n_k, _, tk = vt_ref.shape[1:]
    q2t_f32 = _transposed_queries(q_ref)
    shift = jnp.sqrt(jnp.sum(q2t_f32 * q2t_f32, axis=0, keepdims=True)) * kmax_ref[0]
    q2t = q2t_f32.astype(BF16)
    n_buf = len(p_bufs)
    lag = n_buf - 1

    def probs(j, slot):
        start = pl.multiple_of(j * tk, tk)
        pt = jnp.exp2(_dot(k_ref[pl.ds(start, tk), :], q2t) - shift)
        p_bufs[slot][...] = pt.astype(BF16)
        return jnp.sum(pt, axis=0, keepdims=True)

    def values(j, slot):
        return _dot(vt_ref[0, j], p_bufs[slot][...])

    def step(j, r, l, acc):
        return l + probs(j + lag, (r + lag) % n_buf), acc + values(j, r)

    l = jnp.zeros_like(shift)
    acc = jnp.zeros(q2t_f32.shape, F32)
    for j in range(lag):
        l = l + probs(j, j)
    n_steady = n_k - lag
    n_groups = n_steady // n_buf

    def group(g, carry):
        l, acc = carry
        for r in range(n_buf):
            l, acc = step(g * n_buf + r, r, l, acc)
        return l, acc

    l, acc = lax.fori_loop(0, n_groups, group, (l, acc))
    for j in range(n_groups * n_buf, n_steady):
        l, acc = step(j, j % n_buf, l, acc)
    for j in range(n_steady, n_k):
        acc = acc + values(j, j % n_buf)
    _store_heads(o_ref, acc / l)


def _flash(q, k, vt, *, n_seq, seq, tq=512):
    tk = vt.shape[-1]
    q_blocks = seq // tq
    m = 2 * tq
    norm_max = lambda x: jnp.sqrt(jnp.max(jnp.sum(
        jnp.square(x.astype(F32).reshape(-1, GQA_HEAD_DIM)), axis=-1)))
    kmax = norm_max(k) * FLASH_NORM_SLACK
    specs = [pl.BlockSpec((tq, 2 * LANES), lambda b, g, i: (b * q_blocks + i, g)),
             pl.BlockSpec((seq, LANES), lambda b, g, i: (b, g)),
             pl.BlockSpec((1, seq // tk, LANES, tk), lambda b, g, i: (g, b, 0, 0))]
    common = dict(
        grid=(n_seq, GQA_KV_HEADS, q_blocks),
        out_specs=specs[0],
        out_shape=jax.ShapeDtypeStruct(q.shape, BF16),
        compiler_params=pltpu.CompilerParams(
            dimension_semantics=("parallel", "parallel", "arbitrary"), vmem_limit_bytes=VMEM_LIMIT))

    def bounded(kmax, q, k, vt):
        return pl.pallas_call(
            _flash_bounded_kernel,
            in_specs=[pl.BlockSpec(memory_space=pltpu.SMEM)] + specs,
            scratch_shapes=[pltpu.VMEM((tk, m), BF16)] * FLASH_P_BUFFERS,
            name=f"flash_bounded_{n_seq}x{seq}", **common)(kmax.reshape(1), q, k, vt)

    def online(kmax, q, k, vt):
        return pl.pallas_call(
            _flash_online_kernel,
            in_specs=specs,
            scratch_shapes=[pltpu.VMEM((tk, m), F32)] * 2 + [pltpu.VMEM((tk, m), BF16)] * 2,
            name=f"flash_online_{n_seq}x{seq}", **common)(q, k, vt)

    return lax.cond(norm_max(q) * kmax <= FLASH_SAFE_SHIFT, bounded, online, kmax, q, k, vt)


def kernel(x_prompt, x_sample, p_prompt, p_sample, mix_pre_norm, mix_post_norm, ffn_pre_norm, ffn_post_norm,
           na_w_qkv, na_rpb, na_w_o, gqa_w_qkv, gqa_q_norm, gqa_k_norm, gqa_w_o, ffn_w_gate_up, ffn_w_down,
           ple_w_gate, ple_w_proj):
    gain = lambda g: g.reshape(1, -1)
    layer_w = [dict(wo=(na_w_o[0] if i == 0 else gqa_w_o[0]).astype(BF16),
                    g_post=gain(mix_post_norm[i]), g_ffn_pre=gain(ffn_pre_norm[i]),
                    w_gu=ffn_w_gate_up[i].astype(BF16), w_down=ffn_w_down[i].astype(BF16),
                    g_ffn_post=gain(ffn_post_norm[i]),
                    w_ple_gate=ple_w_gate[i].astype(BF16), w_ple_proj=ple_w_proj[i].astype(BF16))
               for i in range(2)]
    cos, sin = _rope_tables(max(x_prompt.shape[1], x_sample.shape[1]))
    gqa = dict(g_pre=gain(mix_pre_norm[1]), w_qkv=gqa_w_qkv[0].astype(BF16),
               q_norm=gain(gqa_q_norm[0]), k_norm=gain(gqa_k_norm[0]), cos=cos, sin=sin)
    g_pre_na = gain(mix_pre_norm[0])
    w_qkv_na = na_w_qkv[0].astype(BF16)
    bias = _na_bias_table(na_rpb[0])

    def trunk(x, p):
        n_seq, seq, _ = x.shape
        t = n_seq * seq
        h = x.reshape(t, D_MODEL)
        p = p.reshape(p.shape[0], t, PLE_DIM)
        o = _na_attention(_pre_na(h, g_pre_na, w_qkv_na), bias, seq)
        h, q, k, v = _post(o, h, p, 0, layer_w[0], seq=seq, o_is_paired=True, gqa=gqa)
        vt = v.reshape(t // FLASH_TK, FLASH_TK, GQA_KV_HEADS, GQA_HEAD_DIM).transpose(2, 0, 3, 1)
        o = _flash(q, k, vt, n_seq=n_seq, seq=seq)
        (h,) = _post(o, h, p, 1, layer_w[1], seq=seq, o_is_paired=False)
        return h.reshape(x.shape)

    return (trunk(x_prompt, p_prompt), trunk(x_sample, p_sample))
```

```python
import functools
import math

import jax
import jax.numpy as jnp
import numpy as np
from jax import lax
from jax.experimental import pallas as pl
from jax.experimental.pallas import tpu as pltpu

F32 = jnp.float32
BF16 = jnp.bfloat16

D_MODEL = 1024
GRID_W = 64
NA_HEADS = 16
NA_HEAD_DIM = 64
NA_WIN_ROWS = 8
NA_WIN_COLS = 16
GQA_Q_HEADS = 8
GQA_KV_HEADS = 4
GQA_HEAD_DIM = 128
ROPE_THETA = 10000.0
PLE_DIM = 256
D_FF = 2816
EPS = 1e-6
NEG_INF = -1e30
LOG2E = math.log2(math.e)

LANES = 128
VMEM_LIMIT = 56 * 1024 * 1024

NA_Q_ROWS = 4
NA_K_ROWS = 12
NA_TQ = NA_Q_ROWS * GRID_W
NA_TK = NA_K_ROWS * GRID_W
NA_PAIRS = NA_HEADS // 2
POST_ROW_SPLIT = 1
FLASH_TK = 512
FLASH_P_BUFFERS = 3
FLASH_NORM_SLACK = 1.02
FLASH_SAFE_SHIFT = 40.0


def _rms(x):
    return x * lax.rsqrt(jnp.mean(x * x, axis=-1, keepdims=True) + EPS)


def _sigmoid(x):
    return 1.0 / (1.0 + jnp.exp(-x))


def _dot(a, b):
    return jnp.dot(a, b, preferred_element_type=F32)


def _dot_nt(a, b):
    return lax.dot_general(a, b, (((1,), (1,)), ((), ())), preferred_element_type=F32)


def _resident(shape):
    return pl.BlockSpec(shape, lambda *_: (0,) * len(shape), pipeline_mode=pl.Buffered(1))


def _pre_na_kernel(x_ref, g_ref, w_ref, qk_ref, vt_ref):
    a = (_rms(x_ref[...]) * g_ref[...]).astype(BF16)
    n_qk = 2 * NA_PAIRS
    v = _dot(a, w_ref[:, n_qk * LANES:])
    for c in range(NA_PAIRS):
        vt_ref[c] = v[:, c * LANES:(c + 1) * LANES].T.astype(BF16)
    qk = _dot(a, w_ref[:, :n_qk * LANES])
    q_scale = NA_HEAD_DIM ** -0.5 * LOG2E
    for c in range(n_qk):
        piece = qk[:, c * LANES:(c + 1) * LANES]
        if c < NA_PAIRS:
            piece = piece * q_scale
        qk_ref[c] = piece.astype(BF16)


def _pre_na(x, gain, w_qkv, tm=512):
    t = x.shape[0]
    return pl.pallas_call(
        _pre_na_kernel,
        grid=(t // tm,),
        in_specs=[
            pl.BlockSpec((tm, D_MODEL), lambda i: (i, 0)),
            _resident((1, D_MODEL)),
            _resident((D_MODEL, 3 * D_MODEL)),
        ],
        out_specs=[pl.BlockSpec((2 * NA_PAIRS, tm, LANES), lambda i: (0, i, 0)),
                   pl.BlockSpec((NA_PAIRS, LANES, tm), lambda i: (0, 0, i))],
        out_shape=[jax.ShapeDtypeStruct((2 * NA_PAIRS, t, LANES), BF16),
                   jax.ShapeDtypeStruct((NA_PAIRS, LANES, t), BF16)],
        compiler_params=pltpu.CompilerParams(
            dimension_semantics=("parallel",), vmem_limit_bytes=VMEM_LIMIT),
        name=f"pre_na_{t}",
    )(x, gain, w_qkv)


def _na_bias_table(rpb):
    qc = np.arange(GRID_W)[:, None]
    kc = np.arange(GRID_W)[None, :]
    wstart = np.clip(qc - NA_WIN_COLS // 2, 0, GRID_W - NA_WIN_COLS)
    col_ok = (kc >= wstart) & (kc < wstart + NA_WIN_COLS)
    col_off = np.clip(kc - qc + NA_WIN_COLS - 1, 0, 2 * NA_WIN_COLS - 2)
    n_dc = 2 * NA_WIN_COLS - 1
    n_dr = 2 * NA_WIN_ROWS - 1
    onehot = (col_off.T[None] == np.arange(n_dc)[:, None, None]).astype(np.float32)
    cols = jnp.einsum("hdc,ckq->hdkq", rpb, jnp.asarray(onehot), precision=lax.Precision.HIGHEST)
    cols = jnp.where(jnp.asarray(col_ok.T), cols * LOG2E, NEG_INF)
    cols = jnp.concatenate([cols, jnp.full((NA_HEADS, 1, GRID_W, GRID_W), NEG_INF, F32)], axis=1)
    rq = np.arange(NA_Q_ROWS)[:, None]
    u = np.arange(NA_K_ROWS)[None, :]
    row_blocks = []
    for variant in range(3):
        if variant == 0:
            dr = u - rq + NA_WIN_ROWS - 1
            row_ok = np.broadcast_to(u < NA_WIN_ROWS, dr.shape)
        elif variant == 1:
            dr = u - rq + NA_WIN_ROWS - 1 - NA_Q_ROWS
            row_ok = (u >= rq) & (u < rq + NA_WIN_ROWS)
        else:
            dr = u - rq + NA_WIN_ROWS - 1 - 2 * NA_Q_ROWS
            row_ok = np.broadcast_to(u >= NA_K_ROWS - NA_WIN_ROWS, dr.shape)
        dr = np.where(row_ok, dr, n_dr)
        assert dr.min() >= 0 and dr.max() <= n_dr
        row_blocks.append(dr)
    row_blocks = np.stack(row_blocks).reshape(-1)
    vals = jnp.take(cols, jnp.asarray(row_blocks), axis=1)
    vals = vals.reshape(NA_PAIRS, 2, 3, NA_Q_ROWS, NA_K_ROWS, GRID_W, GRID_W).transpose(2, 0, 4, 5, 1, 3, 6)
    return vals.reshape(3, NA_PAIRS, NA_TK, 2 * NA_TQ)


def _na_kernel(q_ref, k0_ref, k1_ref, k2_ref, vt0_ref, vt1_ref, vt2_ref, bias_ref, o_ref,
               s0_ref, s1_ref, p0_ref, p1_ref):
    first_head_lane = lax.broadcasted_iota(jnp.int32, (1, LANES), 1) < NA_HEAD_DIM
    first_head_row = lax.broadcasted_iota(jnp.int32, (LANES, 1), 0) < NA_HEAD_DIM
    k_refs = (k0_ref, k1_ref, k2_ref)
    vt_refs = (vt0_ref, vt1_ref, vt2_ref)
    s_bufs = (s0_ref, s1_ref)
    p_bufs = (p0_ref, p1_ref)
    n_blk = len(k_refs)
    tkb = NA_TK // n_blk

    def scores(hp, slot):
        q = q_ref[hp]
        zero = jnp.zeros_like(q)
        q2 = jnp.concatenate([jnp.where(first_head_lane, q, zero), jnp.where(first_head_lane, zero, q)], axis=0)
        m = None
        for j in range(n_blk):
            rows = slice(j * tkb, (j + 1) * tkb)
            s = _dot_nt(k_refs[j][hp], q2) + bias_ref[0, hp, rows, :]
            s_bufs[slot][rows, :] = s
            mj = jnp.max(s, axis=0, keepdims=True)
            m = mj if m is None else jnp.maximum(m, mj)
        return m

    def softmax(slot, m):
        pt = jnp.exp2(s_bufs[slot][...] - m)
        p_bufs[slot][...] = pt.astype(BF16)
        return jnp.sum(pt, axis=0, keepdims=True)

    def values(hp, slot, l):
        ot = functools.reduce(jnp.add, [
            _dot(vt_refs[j][hp], p_bufs[slot][j * tkb:(j + 1) * tkb, :]) for j in range(n_blk)])
        ot = ot / l
        pair_t = jnp.where(first_head_row, ot[:, :NA_TQ], ot[:, NA_TQ:])
        o_ref[hp] = pair_t.T.astype(BF16)

    m = {}
    l = {}
    for step in range(NA_PAIRS + 2):
        if step < NA_PAIRS:
            m[step] = scores(step, step % 2)
        if 1 <= step <= NA_PAIRS:
            l[step - 1] = softmax((step - 1) % 2, m.pop(step - 1))
        if step >= 2:
            values(step - 2, step % 2, l.pop(step - 2))


def _na_geometry(i, n_blk):
    local = i % n_blk
    win = i - local + jnp.clip(local - 1, 0, n_blk - 3)
    variant = jnp.where(local == 0, 0, jnp.where(local == n_blk - 1, 2, 1))
    return win, variant


def _na_attention(qk, vt, bias, seq):
    t = qk.shape[1]
    assert seq // NA_TQ >= 3
    geom = functools.partial(_na_geometry, n_blk=seq // NA_TQ)
    n_win = NA_TK // NA_TQ

    def k_spec(j):
        return pl.BlockSpec((NA_PAIRS, NA_TQ, LANES), lambda i: (1, geom(i)[0] + j, 0))

    def vt_spec(j):
        return pl.BlockSpec((NA_PAIRS, LANES, NA_TQ), lambda i: (0, 0, geom(i)[0] + j))

    return pl.pallas_call(
        _na_kernel,
        grid=(t // NA_TQ,),
        in_specs=[pl.BlockSpec((NA_PAIRS, NA_TQ, LANES), lambda i: (0, i, 0))]
        + [k_spec(j) for j in range(n_win)] + [vt_spec(j) for j in range(n_win)]
        + [pl.BlockSpec((1, NA_PAIRS, NA_TK, 2 * NA_TQ), lambda i: (geom(i)[1], 0, 0, 0))],
        out_specs=pl.BlockSpec((NA_PAIRS, NA_TQ, LANES), lambda i: (0, i, 0)),
        out_shape=jax.ShapeDtypeStruct((NA_PAIRS, t, LANES), BF16),
        scratch_shapes=[pltpu.VMEM((NA_TK, 2 * NA_TQ), F32)] * 2 + [pltpu.VMEM((NA_TK, 2 * NA_TQ), BF16)] * 2,
        compiler_params=pltpu.CompilerParams(
            dimension_semantics=("parallel",), vmem_limit_bytes=VMEM_LIMIT),
        name=f"na_attention_{t // seq}x{seq}",
    )(qk, *([qk] * n_win), *([vt] * n_win), bias)


def _swap_rope_halves(x, lane_in_first_half):
    return jnp.where(lane_in_first_half, pltpu.roll(x, 96, 1), pltpu.roll(x, 32, 1))


def _post_kernel(*refs, o_is_paired, with_gqa, n_split):
    if with_gqa:
        (o_ref, h_ref, p_ref, wo_ref, gpost_ref, gfpre_ref, wgu_ref, wd_ref, gfpost_ref, wpg_ref, wpp_ref,
         gnext_ref, wqkv_ref, qn_ref, kn_ref, cos_ref, sin_ref, h_out, q_out, k_out, vt_out) = refs
    else:
        (o_ref, h_ref, p_ref, wo_ref, gpost_ref, gfpre_ref, wgu_ref, wd_ref, gfpost_ref, wpg_ref, wpp_ref,
         h_out) = refs

    sub = h_ref.shape[0] // n_split
    for r in range(n_split):
        rows = slice(r * sub, (r + 1) * sub)
        if o_is_paired:
            o = jnp.concatenate([o_ref[c, rows, :] for c in range(NA_PAIRS)], axis=-1)
        else:
            o = o_ref[rows, :]
        h = h_ref[rows, :] + _rms(_dot(o, wo_ref[...])) * gpost_ref[...]

        a = (_rms(h) * gfpre_ref[...]).astype(BF16)
        g = _dot(a, wgu_ref[:, :D_FF])
        u = _dot(a, wgu_ref[:, D_FF:])
        act = (g * _sigmoid(g) * u).astype(BF16)
        h = h + _rms(_dot(act, wd_ref[...])) * gfpost_ref[...]

        gate = _sigmoid(_dot(_rms(h).astype(BF16), wpg_ref[...]))
        h = h + _dot(p_ref[rows, :].astype(BF16), wpp_ref[...]) * gate
        h_out[rows, :] = h

        if with_gqa:
            a = (_rms(h) * gnext_ref[...]).astype(BF16)
            n_qk = GQA_Q_HEADS + GQA_KV_HEADS
            v = _dot(a, wqkv_ref[:, n_qk * LANES:])
            for c in range(GQA_KV_HEADS):
                vt_out[c, 0, :, rows] = v[:, c * LANES:(c + 1) * LANES].T.astype(BF16)
            qk = _dot(a, wqkv_ref[:, :n_qk * LANES])
            cos = cos_ref[rows, :]
            sin = sin_ref[rows, :]
            lane = lax.broadcasted_iota(jnp.int32, (1, LANES), 1)
            first_half = (lane % 64) < 32
            q_scale = GQA_HEAD_DIM ** -0.5 * LOG2E
            for c in range(n_qk):
                x = qk[:, c * LANES:(c + 1) * LANES]
                is_q = c < GQA_Q_HEADS
                x = _rms(x) * (qn_ref[...] if is_q else kn_ref[...])
                x = x * cos + _swap_rope_halves(x, first_half) * sin
                if is_q:
                    q_out[rows, c * LANES:(c + 1) * LANES] = (x * q_scale).astype(BF16)
                else:
                    cc = c - GQA_Q_HEADS
                    k_out[rows, cc * LANES:(cc + 1) * LANES] = x.astype(BF16)


def _rope_tables(seq):
    rows = seq // GRID_W
    axis_dim = GQA_HEAD_DIM // 2
    inv_freq = ROPE_THETA ** (-jnp.arange(0, axis_dim, 2, dtype=F32) / axis_dim)
    n_f = inv_freq.shape[0]
    ang_r = jnp.arange(rows, dtype=F32)[:, None] * inv_freq
    ang_c = jnp.arange(GRID_W, dtype=F32)[:, None] * inv_freq
    by_row = lambda a: jnp.broadcast_to(a[:, None, :], (rows, GRID_W, n_f))
    by_col = lambda a: jnp.broadcast_to(a[None, :, :], (rows, GRID_W, n_f))
    cos_r, sin_r, cos_c, sin_c = (by_row(jnp.cos(ang_r)), by_row(jnp.sin(ang_r)),
                                  by_col(jnp.cos(ang_c)), by_col(jnp.sin(ang_c)))
    cos_t = jnp.concatenate([cos_r, cos_r, cos_c, cos_c], axis=-1).reshape(seq, GQA_HEAD_DIM)
    sin_t = jnp.concatenate([-sin_r, sin_r, -sin_c, sin_c], axis=-1).reshape(seq, GQA_HEAD_DIM)
    return cos_t, sin_t


def _post(o, h, p, layer, w, *, seq, o_is_paired, gqa=None, tm=256):
    t = h.shape[0]
    row = lambda width: pl.BlockSpec((tm, width), lambda i: (i, 0))
    o_spec = (pl.BlockSpec((NA_PAIRS, tm, LANES), lambda i: (0, i, 0)) if o_is_paired else row(D_MODEL))
    in_specs = [o_spec, row(D_MODEL), pl.BlockSpec((None, tm, PLE_DIM), lambda i: (layer, i, 0)),
                _resident((D_MODEL, D_MODEL)), _resident((1, D_MODEL)), _resident((1, D_MODEL)),
                _resident((D_MODEL, 2 * D_FF)), _resident((D_FF, D_MODEL)), _resident((1, D_MODEL)),
                _resident((D_MODEL, D_MODEL)), _resident((PLE_DIM, D_MODEL))]
    args = [o, h, p, w["wo"], w["g_post"], w["g_ffn_pre"], w["w_gu"], w["w_down"], w["g_ffn_post"],
            w["w_ple_gate"], w["w_ple_proj"]]
    out_specs = [row(D_MODEL)]
    out_shape = [jax.ShapeDtypeStruct((t, D_MODEL), F32)]
    if gqa is not None:
        kv_width = GQA_KV_HEADS * GQA_HEAD_DIM
        per_seq = seq // tm

        def pos_block(i):
            return (i % per_seq, 0)

        in_specs += [_resident((1, D_MODEL)), _resident((D_MODEL, D_MODEL + 2 * kv_width)),
                     _resident((1, LANES)), _resident((1, LANES)),
                     pl.BlockSpec((tm, LANES), pos_block), pl.BlockSpec((tm, LANES), pos_block)]
        args += [gqa["g_pre"], gqa["w_qkv"], gqa["q_norm"], gqa["k_norm"], gqa["cos"], gqa["sin"]]
        per_kb = FLASH_TK // tm
        out_specs += [row(D_MODEL), row(kv_width),
                      pl.BlockSpec((GQA_KV_HEADS, 1, LANES, tm), lambda i: (0, i // per_kb, 0, i % per_kb))]
        out_shape += [jax.ShapeDtypeStruct((t, D_MODEL), BF16),
                      jax.ShapeDtypeStruct((t, kv_width), BF16),
                      jax.ShapeDtypeStruct((GQA_KV_HEADS, t // FLASH_TK, LANES, FLASH_TK), BF16)]
    return pl.pallas_call(
        functools.partial(_post_kernel, o_is_paired=o_is_paired, with_gqa=gqa is not None,
                          n_split=POST_ROW_SPLIT),
        grid=(t // tm,),
        in_specs=in_specs,
        out_specs=out_specs,
        out_shape=out_shape,
        compiler_params=pltpu.CompilerParams(
            dimension_semantics=("parallel",), vmem_limit_bytes=VMEM_LIMIT),
        name=("post_gqa" if gqa is not None else "post_final") + f"_{t // seq}x{seq}",
    )(*args)


def _transposed_queries(q_ref):
    q = q_ref[...].astype(F32)
    return jnp.concatenate([q[:, :LANES].T, q[:, LANES:].T], axis=1)


def _store_heads(o_ref, ot):
    tq = o_ref.shape[0]
    o_ref[...] = jnp.concatenate([ot[:, :tq].T, ot[:, tq:].T], axis=1).astype(BF16)


def _flash_online_kernel(q_ref, k_ref, vt_ref, o_ref, s0_ref, s1_ref, p0_ref, p1_ref):
    tq = q_ref.shape[0]
    n_k, _, tk = vt_ref.shape[1:]
    q2t = _transposed_queries(q_ref).astype(BF16)
    s_bufs = (s0_ref, s1_ref)
    p_bufs = (p0_ref, p1_ref)

    def scores(j, slot):
        start = pl.multiple_of(j * tk, tk)
        s_bufs[slot][...] = _dot(k_ref[pl.ds(start, tk), :], q2t)

    def softmax(slot, m, l):
        st = s_bufs[slot][...]
        m_new = jnp.maximum(m, jnp.max(st, axis=0, keepdims=True))
        alpha = jnp.exp2(m - m_new)
        pt = jnp.exp2(st - m_new)
        p_bufs[slot][...] = pt.astype(BF16)
        return m_new, alpha * l + jnp.sum(pt, axis=0, keepdims=True), alpha

    def values(j, slot, alpha, acc):
        return alpha * acc + _dot(vt_ref[0, j], p_bufs[slot][...])

    m = jnp.full((1, 2 * tq), NEG_INF, F32)
    l = jnp.zeros((1, 2 * tq), F32)
    acc = jnp.zeros((LANES, 2 * tq), F32)
    scores(0, 0)
    scores(1, 1)
    m, l, alpha = softmax(0, m, l)

    def two_steps(jj, carry):
        m, l, alpha_prev, acc = carry
        j = 2 * jj + 1
        scores(j + 1, 0)
        m, l, alpha = softmax(1, m, l)
        acc = values(j - 1, 0, alpha_prev, acc)
        scores(j + 2, 1)
        m, l, alpha_next = softmax(0, m, l)
        acc = values(j, 1, alpha, acc)
        return m, l, alpha_next, acc

    assert n_k >= 2 and n_k % 2 == 0
    m, l, alpha_prev, acc = lax.fori_loop(0, (n_k - 2) // 2, two_steps, (m, l, alpha, acc))
    m, l, alpha = softmax(1, m, l)
    acc = values(n_k - 2, 0, alpha_prev, acc)
    acc = values(n_k - 1, 1, alpha, acc)
    _store_heads(o_ref, acc / l)


def _flash_bounded_kernel(kmax_ref, q_ref, k_ref, vt_ref, o_ref, *p_bufs):
    n_k, _, tk = vt_ref.shape[1:]
    q2t_f32 = _transposed_queries(q_ref)
    shift = jnp.sqrt(jnp.sum(q2t_f32 * q2t_f32, axis=0, keepdims=True)) * kmax_ref[0]
    q2t = q2t_f32.astype(BF16)
    n_buf = len(p_bufs)
    lag = n_buf - 1

    def probs(j, slot):
        start = pl.multiple_of(j * tk, tk)
        pt = jnp.exp2(_dot(k_ref[pl.ds(start, tk), :], q2t) - shift)
        p_bufs[slot][...] = pt.astype(BF16)
        return jnp.sum(pt, axis=0, keepdims=True)

    def values(j, slot):
        return _dot(vt_ref[0, j], p_bufs[slot][...])

    def step(j, r, l, acc):
        return l + probs(j + lag, (r + lag) % n_buf), acc + values(j, r)

    l = jnp.zeros_like(shift)
    acc = jnp.zeros(q2t_f32.shape, F32)
    for j in range(lag):
        l = l + probs(j, j)
    n_steady = n_k - lag
    n_groups = n_steady // n_buf

    def group(g, carry):
        l, acc = carry
        for r in range(n_buf):
            l, acc = step(g * n_buf + r, r, l, acc)
        return l, acc

    l, acc = lax.fori_loop(0, n_groups, group, (l, acc))
    for j in range(n_groups * n_buf, n_steady):
        l, acc = step(j, j % n_buf, l, acc)
    for j in range(n_steady, n_k):
        acc = acc + values(j, j % n_buf)
    _store_heads(o_ref, acc / l)


def _flash(q, k, vt, q_norm_max, k_norm_max, *, n_seq, seq, tq=1024):
    tk = vt.shape[-1]
    q_blocks = seq // tq
    m = 2 * tq
    kmax = k_norm_max
    specs = [pl.BlockSpec((tq, 2 * LANES), lambda b, g, i: (b * q_blocks + i, g)),
             pl.BlockSpec((seq, LANES), lambda b, g, i: (b, g)),
             pl.BlockSpec((1, seq // tk, LANES, tk), lambda b, g, i: (g, b, 0, 0))]
    common = dict(
        grid=(n_seq, GQA_KV_HEADS, q_blocks),
        out_specs=specs[0],
        out_shape=jax.ShapeDtypeStruct(q.shape, BF16),
        compiler_params=pltpu.CompilerParams(
            dimension_semantics=("parallel", "parallel", "arbitrary"), vmem_limit_bytes=VMEM_LIMIT))

    def bounded(kmax, q, k, vt):
        return pl.pallas_call(
            _flash_bounded_kernel,
            in_specs=[pl.BlockSpec(memory_space=pltpu.SMEM)] + specs,
            scratch_shapes=[pltpu.VMEM((tk, m), BF16)] * FLASH_P_BUFFERS,
            name=f"flash_bounded_{n_seq}x{seq}", **common)(kmax.reshape(1), q, k, vt)

    def online(kmax, q, k, vt):
        return pl.pallas_call(
            _flash_online_kernel,
            in_specs=specs,
            scratch_shapes=[pltpu.VMEM((tk, m), F32)] * 2 + [pltpu.VMEM((tk, m), BF16)] * 2,
            name=f"flash_online_{n_seq}x{seq}", **common)(q, k, vt)

    return lax.cond(q_norm_max * kmax <= FLASH_SAFE_SHIFT, bounded, online, kmax, q, k, vt)


def kernel(x_prompt, x_sample, p_prompt, p_sample, mix_pre_norm, mix_post_norm, ffn_pre_norm, ffn_post_norm,
           na_w_qkv, na_rpb, na_w_o, gqa_w_qkv, gqa_q_norm, gqa_k_norm, gqa_w_o, ffn_w_gate_up, ffn_w_down,
           ple_w_gate, ple_w_proj):
    gain = lambda g: g.reshape(1, -1)
    layer_w = [dict(wo=(na_w_o[0] if i == 0 else gqa_w_o[0]).astype(BF16),
                    g_post=gain(mix_post_norm[i]), g_ffn_pre=gain(ffn_pre_norm[i]),
                    w_gu=ffn_w_gate_up[i].astype(BF16), w_down=ffn_w_down[i].astype(BF16),
                    g_ffn_post=gain(ffn_post_norm[i]),
                    w_ple_gate=ple_w_gate[i].astype(BF16), w_ple_proj=ple_w_proj[i].astype(BF16))
               for i in range(2)]
    cos, sin = _rope_tables(max(x_prompt.shape[1], x_sample.shape[1]))
    gqa = dict(g_pre=gain(mix_pre_norm[1]), w_qkv=gqa_w_qkv[0].astype(BF16),
               q_norm=gain(gqa_q_norm[0]), k_norm=gain(gqa_k_norm[0]), cos=cos, sin=sin)
    unit_norm = math.sqrt(GQA_HEAD_DIM) * FLASH_NORM_SLACK
    q_norm_max = unit_norm * jnp.max(jnp.abs(gqa_q_norm[0])) * (GQA_HEAD_DIM ** -0.5 * LOG2E)
    k_norm_max = unit_norm * jnp.max(jnp.abs(gqa_k_norm[0]))
    g_pre_na = gain(mix_pre_norm[0])
    w_qkv_na = na_w_qkv[0].astype(BF16)
    bias = _na_bias_table(na_rpb[0])

    def trunk(x, p):
        n_seq, seq, _ = x.shape
        t = n_seq * seq
        h = x.reshape(t, D_MODEL)
        p = p.reshape(p.shape[0], t, PLE_DIM)
        o = _na_attention(*_pre_na(h, g_pre_na, w_qkv_na), bias, seq)
        h, q, k, vt = _post(o, h, p, 0, layer_w[0], seq=seq, o_is_paired=True, gqa=gqa)
        o = _flash(q, k, vt, q_norm_max, k_norm_max, n_seq=n_seq, seq=seq)
        (h,) = _post(o, h, p, 1, layer_w[1], seq=seq, o_is_paired=False)
        return h.reshape(x.shape)

    return (trunk(x_prompt, p_prompt), trunk(x_sample, p_sample))
```
